```python
import jax, jax.numpy as jnp
from jax import lax
import numpy as np

D_MODEL = 1024
BATCH = 4
SEQ = 8192
DEPTH = 1

MIX_WIDTH = D_MODEL
CONV_WIDTH = MIX_WIDTH // 2
CONV_GROUPS = 8
CONV_K = 3
RET_WIDTH = MIX_WIDTH - CONV_WIDTH
RET_HEADS = 4
RET_HEAD_DIM = RET_WIDTH // RET_HEADS
RET_CHUNK = 128
ROPE_BASE = 10000.0
D_FF = 4 * D_MODEL
NORM_EPS = 1e-6
IN_COLS = 3 * CONV_WIDTH + 4 * RET_WIDTH

kernel_name = "hymba_conv_retention_hybrid"


def rms_norm(x, g, eps=NORM_EPS):
    xf = x.astype(jnp.float32)
    y = xf * lax.rsqrt(jnp.mean(xf * xf, axis=-1, keepdims=True) + eps)
    return (y * g.astype(jnp.float32)).astype(x.dtype)


def group_rms_norm(x, g, n_groups, eps=NORM_EPS):
    shp = x.shape
    xg = x.reshape(shp[:-1] + (n_groups, shp[-1] // n_groups)).astype(jnp.float32)
    y = xg * lax.rsqrt(jnp.mean(xg * xg, axis=-1, keepdims=True) + eps)
    return (y.reshape(shp) * g.astype(jnp.float32)).astype(x.dtype)


def rotary(x, positions):
    half = x.shape[-1] // 2
    inv_freq = 1.0 / (ROPE_BASE ** (jnp.arange(half, dtype=jnp.float32) / half))
    ang = positions.astype(jnp.float32)[:, None] * inv_freq[None, :]
    cos = jnp.cos(ang)[None, :, None, :].astype(x.dtype)
    sin = jnp.sin(ang)[None, :, None, :].astype(x.dtype)
    x1, x2 = x[..., :half], x[..., half:]
    return jnp.concatenate([x1 * cos - x2 * sin, x2 * cos + x1 * sin], axis=-1)


def causal_dwconv3(u, w):
    up = jnp.pad(u, ((0, 0), (CONV_K - 1, 0), (0, 0)))
    s = u.shape[1]
    return up[:, 0:s] * w[0] + up[:, 1:s + 1] * w[1] + up[:, 2:s + 2] * w[2]


def chunkwise_retention(q, k, v):
    b, s, h, d = q.shape
    nc = s // RET_CHUNK
    dt = q.dtype
    to_chunks = lambda t: t.reshape(b, nc, RET_CHUNK, h, d).transpose(0, 3, 1, 2, 4)
    qc, kc, vc = to_chunks(q), to_chunks(k), to_chunks(v)

    log_gamma = jnp.log(1.0 - 2.0 ** (-5.0 - jnp.arange(h, dtype=jnp.float32)))
    idx = jnp.arange(RET_CHUNK, dtype=jnp.float32)
    diff = idx[:, None] - idx[None, :]
    intra_decay = jnp.where(diff[None] >= 0,
                            jnp.exp(log_gamma[:, None, None] * jnp.maximum(diff, 0.0)[None]),
                            0.0).astype(dt)
    zeta = jnp.exp(log_gamma[:, None] * (RET_CHUNK - 1 - idx)[None]).astype(dt)
    xi = jnp.exp(log_gamma[:, None] * (idx + 1.0)[None]).astype(dt)
    chunk_decay = jnp.exp(log_gamma * RET_CHUNK).astype(dt)

    scores = jnp.einsum('bhncd,bhnmd->bhncm', qc, kc) * intra_decay[None, :, None]
    o_intra = jnp.einsum('bhncm,bhnme->bhnce', scores, vc)

    kv = jnp.einsum('bhnmd,bhnme->nbhde', kc * zeta[None, :, None, :, None], vc)

    def step(state, kv_n):
        new_state = chunk_decay[None, :, None, None] * state + kv_n
        return new_state, state

    _, s_prev = lax.scan(step, jnp.zeros_like(kv[0]), kv)
    o_cross = jnp.einsum('bhncd,nbhde->bhnce', qc * xi[None, :, None, :, None], s_prev)

    o = o_intra + o_cross
    return o.transpose(0, 2, 3, 1, 4).reshape(b, s, h, d)


def hybrid_mixer(u, w_in, conv_w, conv_norm_g, ret_norm_g, w_out):
    b, s, _ = u.shape
    z = u @ w_in
    c0 = 3 * CONV_WIDTH
    cb, cc, ch = jnp.split(z[..., :c0], 3, axis=-1)
    rq, rk, rv, rg = jnp.split(z[..., c0:], 4, axis=-1)

    y_conv = cb * causal_dwconv3(cc * ch, conv_w)
    y_conv = group_rms_norm(y_conv, conv_norm_g, CONV_GROUPS)

    positions = jnp.arange(s)
    q = rotary(rq.reshape(b, s, RET_HEADS, RET_HEAD_DIM), positions)
    k = rotary(rk.reshape(b, s, RET_HEADS, RET_HEAD_DIM), positions) * (RET_HEAD_DIM ** -0.5)
    v = rv.reshape(b, s, RET_HEADS, RET_HEAD_DIM)
    o = chunkwise_retention(q, k, v).reshape(b, s, RET_WIDTH)
    y_ret = group_rms_norm(o, ret_norm_g, RET_HEADS) * jax.nn.silu(rg)

    return jnp.concatenate([y_conv, y_ret], axis=-1) @ w_out


def sq_relu_mlp(u, w_up, w_down):
    hdn = jax.nn.relu(u @ w_up)
    return (hdn * hdn) @ w_down


def setup_inputs(seed: int = 0) -> dict:
    key = jax.random.key(seed)
    ks = jax.random.split(key, 11)
    f32 = jnp.float32
    x = jax.random.normal(ks[0], (BATCH, SEQ, D_MODEL), f32)
    norm1_g = 1.0 + 0.05 * jax.random.normal(ks[1], (D_MODEL,), f32)
    w_in = jax.random.normal(ks[2], (D_MODEL, IN_COLS), f32) * D_MODEL ** -0.5
    conv_w = jax.random.normal(ks[3], (CONV_K, CONV_WIDTH), f32) * CONV_K ** -0.5
    conv_norm_g = 1.0 + 0.05 * jax.random.normal(ks[4], (CONV_WIDTH,), f32)
    ret_norm_g = 1.0 + 0.05 * jax.random.normal(ks[5], (RET_WIDTH,), f32)
    w_out = jax.random.normal(ks[6], (MIX_WIDTH, D_MODEL), f32) * MIX_WIDTH ** -0.5
    norm2_g = 1.0 + 0.05 * jax.random.normal(ks[7], (D_MODEL,), f32)
    w_up = jax.random.normal(ks[8], (D_MODEL, D_FF), f32) * D_MODEL ** -0.5
    w_down = jax.random.normal(ks[9], (D_FF, D_MODEL), f32) * D_FF ** -0.5
    final_norm_g = 1.0 + 0.05 * jax.random.normal(ks[10], (D_MODEL,), f32)
    return {"x": x, "norm1_g": norm1_g, "w_in": w_in, "conv_w": conv_w,
            "conv_norm_g": conv_norm_g, "ret_norm_g": ret_norm_g, "w_out": w_out,
            "norm2_g": norm2_g, "w_up": w_up, "w_down": w_down,
            "final_norm_g": final_norm_g}


def reference(x, norm1_g, w_in, conv_w, conv_norm_g, ret_norm_g, w_out,
              norm2_g, w_up, w_down, final_norm_g):
    h = x
    for _ in range(DEPTH):
        h = h + hybrid_mixer(rms_norm(h, norm1_g), w_in, conv_w, conv_norm_g, ret_norm_g, w_out)
        h = h + sq_relu_mlp(rms_norm(h, norm2_g), w_up, w_down)
    return rms_norm(h, final_norm_g)
```

```python
import functools

import jax
import jax.numpy as jnp
from jax import lax
from jax.experimental import pallas as pl
from jax.experimental.pallas import tpu as pltpu

D_MODEL = 1024
CONV_WIDTH = 512
CONV_GROUPS = 8
CONV_GROUP_DIM = CONV_WIDTH // CONV_GROUPS
CONV_K = 3
RET_WIDTH = 512
RET_HEADS = 4
HEAD_DIM = RET_WIDTH // RET_HEADS
CHUNK = 128
ROPE_BASE = 10000.0
D_FF = 4 * D_MODEL
NORM_EPS = 1e-6
IN_COLS = 3 * CONV_WIDTH + 4 * RET_WIDTH

LANES = 128
SUBLANES = 8
VMEM_LIMIT_BYTES = 56 * 1024 * 1024

MIXER_TILE = 512
MLP_TILE = 512
FF_CHUNK = 1024

BF16 = jnp.bfloat16
F32 = jnp.float32


def _dot(a, b):
    return jnp.dot(a, b, preferred_element_type=F32)


def _rms_scale(x):
    return lax.rsqrt(jnp.mean(x * x, axis=-1, keepdims=True) + NORM_EPS)


def _mixer_kernel(cd_ref, x_ref, g1_ref, win_ref, convw_ref, cng_ref, rng_ref, wout_ref,
                  cos_ref, sin_ref, decay_ref, zeta_ref, xi_ref,
                  h_ref, state_ref, pext_ref, mix_ref):
    tile = x_ref.shape[0]

    @pl.when(pl.program_id(1) == 0)
    def _():
        state_ref[...] = jnp.zeros_like(state_ref)
        pext_ref[0:SUBLANES, :] = jnp.zeros((SUBLANES, CONV_WIDTH), F32)

    x = x_ref[...]
    u = (x * _rms_scale(x) * g1_ref[...]).astype(BF16)

    cb = _dot(u, win_ref[:, 0:CONV_WIDTH])
    cc = _dot(u, win_ref[:, CONV_WIDTH:2 * CONV_WIDTH])
    ch = _dot(u, win_ref[:, 2 * CONV_WIDTH:3 * CONV_WIDTH])
    p = cc * ch
    pext_ref[SUBLANES:SUBLANES + tile, :] = p
    p1 = pext_ref[SUBLANES - 1:SUBLANES - 1 + tile, :]
    p2 = pext_ref[SUBLANES - 2:SUBLANES - 2 + tile, :]
    y = cb * (p2 * convw_ref[0:1, :] + p1 * convw_ref[1:2, :] + p * convw_ref[2:3, :])
    pext_ref[0:SUBLANES, :] = p[tile - SUBLANES:tile, :]

    lane = lax.broadcasted_iota(jnp.int32, (tile, LANES), 1)
    low = lane < CONV_GROUP_DIM
    for blk in range(CONV_WIDTH // LANES):
        sl = slice(blk * LANES, (blk + 1) * LANES)
        yb = y[:, sl]
        y2 = yb * yb
        ss_lo = jnp.sum(jnp.where(low, y2, 0.0), axis=-1, keepdims=True)
        ss_hi = jnp.sum(jnp.where(low, 0.0, y2), axis=-1, keepdims=True)
        inv = lax.rsqrt(jnp.where(low, ss_lo, ss_hi) * (1.0 / CONV_GROUP_DIM) + NORM_EPS)
        mix_ref[:, sl] = (yb * inv * cng_ref[:, sl]).astype(BF16)

    c0 = 3 * CONV_WIDTH
    zq = _dot(u, win_ref[:, c0:c0 + RET_WIDTH])
    zk = _dot(u, win_ref[:, c0 + RET_WIDTH:c0 + 2 * RET_WIDTH])
    zv = _dot(u, win_ref[:, c0 + 2 * RET_WIDTH:c0 + 3 * RET_WIDTH])
    zg = _dot(u, win_ref[:, c0 + 3 * RET_WIDTH:c0 + 4 * RET_WIDTH])
    cos = cos_ref[...]
    sin = sin_ref[...]

    for hd in range(RET_HEADS):
        hs = slice(hd * HEAD_DIM, (hd + 1) * HEAD_DIM)
        qh = zq[:, hs]
        kh = zk[:, hs]
        qh = qh * cos + pltpu.roll(qh, HEAD_DIM // 2, axis=1) * sin
        kh = kh * cos + pltpu.roll(kh, HEAD_DIM // 2, axis=1) * sin
        vh = zv[:, hs].astype(BF16)
        decay = decay_ref[hd]
        zeta = zeta_ref[hd]
        xi = xi_ref[hd]
        cd = cd_ref[hd]
        outs = []
        for c in range(tile // CHUNK):
            rows = slice(c * CHUNK, (c + 1) * CHUNK)
            q = qh[rows]
            k = kh[rows]
            v = vh[rows]
            state = state_ref[hd]
            s = lax.dot_general(q.astype(BF16), k.astype(BF16), (((1,), (1,)), ((), ())),
                                preferred_element_type=F32) * decay
            lhs = jnp.concatenate([s.astype(BF16), (q * xi).astype(BF16)], axis=1)
            rhs = jnp.concatenate([v, state.astype(BF16)], axis=0)
            outs.append(_dot(lhs, rhs))
            kv = lax.dot_general((k * zeta).astype(BF16), v, (((0,), (0,)), ((), ())),
                                 preferred_element_type=F32)
            state_ref[hd] = cd * state + kv
        o = jnp.concatenate(outs, axis=0)
        gate = zg[:, hs]
        gate = gate * (1.0 / (1.0 + jnp.exp(-gate)))
        yr = o * _rms_scale(o) * rng_ref[:, hs] * gate
        mix_ref[:, CONV_WIDTH + hd * HEAD_DIM:CONV_WIDTH + (hd + 1) * HEAD_DIM] = yr.astype(BF16)

    h_ref[...] = x + _dot(mix_ref[...], wout_ref[...])


def _mlp_kernel(h_ref, g2_ref, wup_ref, wdn_ref, gf_ref, o_ref, hid_ref):
    h = h_ref[...]
    u = (h * _rms_scale(h) * g2_ref[...]).astype(BF16)
    for c in range(D_FF // FF_CHUNK):
        cols = slice(c * FF_CHUNK, (c + 1) * FF_CHUNK)
        a = jnp.maximum(_dot(u, wup_ref[:, cols]), 0.0)
        hid_ref[:, cols] = (a * a).astype(BF16)
    y = h + _dot(hid_ref[...], wdn_ref[...])
    o_ref[...] = y * _rms_scale(y) * gf_ref[...]


def _resident(shape):
    nd = len(shape)
    return pl.BlockSpec(shape, lambda *_: (0,) * nd, pipeline_mode=pl.Buffered(1))


def _retention_tables(seq):
    half = HEAD_DIM // 2
    inv_freq = 1.0 / (ROPE_BASE ** (jnp.arange(half, dtype=F32) / half))
    ang = jnp.arange(seq).astype(F32)[:, None] * inv_freq[None, :]
    cos = jnp.cos(ang)
    sin = jnp.sin(ang)
    cos_t = jnp.concatenate([cos, cos], axis=-1)
    sin_t = jnp.concatenate([-sin, sin], axis=-1)

    log_gamma = jnp.log(1.0 - 2.0 ** (-5.0 - jnp.arange(RET_HEADS, dtype=F32)))
    idx = jnp.arange(CHUNK, dtype=F32)
    diff = idx[:, None] - idx[None, :]
    intra = jnp.where(diff[None] >= 0,
                      jnp.exp(log_gamma[:, None, None] * jnp.maximum(diff, 0.0)[None]), 0.0)
    zeta = jnp.exp(log_gamma[:, None] * (CHUNK - 1 - idx)[None])
    xi = jnp.exp(log_gamma[:, None] * (idx + 1.0)[None])
    chunk_decay = jnp.exp(log_gamma * CHUNK)
    k_scale = HEAD_DIM ** -0.5
    decay_t = intra * k_scale
    zeta_t = jnp.broadcast_to((zeta * k_scale)[:, :, None], (RET_HEADS, CHUNK, HEAD_DIM))
    xi_t = jnp.broadcast_to(xi[:, :, None], (RET_HEADS, CHUNK, HEAD_DIM))
    return cos_t, sin_t, decay_t, zeta_t, xi_t, chunk_decay


def kernel(x, norm1_g, w_in, conv_w, conv_norm_g, ret_norm_g, w_out, norm2_g, w_up, w_down, final_norm_g):
    batch, seq, d_model = x.shape
    assert d_model == D_MODEL and w_in.shape == (D_MODEL, IN_COLS)
    assert seq % MIXER_TILE == 0 and MIXER_TILE % CHUNK == 0 and (batch * seq) % MLP_TILE == 0

    cos_t, sin_t, decay_t, zeta_t, xi_t, chunk_decay = _retention_tables(seq)
    row = lambda g: g.reshape(1, -1).astype(F32)

    tile_spec = pl.BlockSpec((None, MIXER_TILE, D_MODEL), lambda b, j: (b, j, 0))
    rope_spec = pl.BlockSpec((MIXER_TILE, HEAD_DIM), lambda b, j: (j, 0))
    h = pl.pallas_call(
        _mixer_kernel,
        grid=(batch, seq // MIXER_TILE),
        in_specs=[
            pl.BlockSpec(memory_space=pltpu.SMEM),
            tile_spec,
            _resident((1, D_MODEL)),
            _resident((D_MODEL, IN_COLS)),
            _resident((CONV_K, CONV_WIDTH)),
            _resident((1, CONV_WIDTH)),
            _resident((1, RET_WIDTH)),
            _resident((D_MODEL, D_MODEL)),
            rope_spec,
            rope_spec,
            _resident((RET_HEADS, CHUNK, CHUNK)),
            _resident((RET_HEADS, CHUNK, HEAD_DIM)),
            _resident((RET_HEADS, CHUNK, HEAD_DIM)),
        ],
        out_specs=tile_spec,
        out_shape=jax.ShapeDtypeStruct(x.shape, F32),
        scratch_shapes=[
            pltpu.VMEM((RET_HEADS, HEAD_DIM, HEAD_DIM), F32),
            pltpu.VMEM((MIXER_TILE + SUBLANES, CONV_WIDTH), F32),
            pltpu.VMEM((MIXER_TILE, D_MODEL), BF16),
        ],
        compiler_params=pltpu.CompilerParams(
            dimension_semantics=("arbitrary", "arbitrary"),
            vmem_limit_bytes=VMEM_LIMIT_BYTES),
    )(chunk_decay, x, row(norm1_g), w_in.astype(BF16), conv_w, row(conv_norm_g), row(ret_norm_g),
      w_out.astype(BF16), cos_t, sin_t, decay_t, zeta_t, xi_t)

    tokens = batch * seq
    tok_spec = pl.BlockSpec((MLP_TILE, D_MODEL), lambda i: (i, 0))
    out = pl.pallas_call(
        _mlp_kernel,
        grid=(tokens // MLP_TILE,),
        in_specs=[
            tok_spec,
            _resident((1, D_MODEL)),
            _resident((D_MODEL, D_FF)),
            _resident((D_FF, D_MODEL)),
            _resident((1, D_MODEL)),
        ],
        out_specs=tok_spec,
        out_shape=jax.ShapeDtypeStruct((tokens, D_MODEL), F32),
        scratch_shapes=[pltpu.VMEM((MLP_TILE, D_FF), BF16)],
        compiler_params=pltpu.CompilerParams(
            dimension_semantics=("arbitrary",),
            vmem_limit_bytes=VMEM_LIMIT_BYTES),
    )(h.reshape(tokens, D_MODEL), row(norm2_g), w_up.astype(BF16), w_down.astype(BF16), row(final_norm_g))
    return out.reshape(batch, seq, D_MODEL)
```

```python
import jax
import jax.numpy as jnp
from jax import lax
from jax.experimental import pallas as pl
from jax.experimental.pallas import tpu as pltpu

D_MODEL = 1024
CONV_WIDTH = 512
CONV_GROUPS = 8
CONV_GROUP_DIM = CONV_WIDTH // CONV_GROUPS
CONV_K = 3
RET_WIDTH = 512
RET_HEADS = 4
HEAD_DIM = RET_WIDTH // RET_HEADS
CHUNK = 128
ROPE_BASE = 10000.0
D_FF = 4 * D_MODEL
NORM_EPS = 1e-6
IN_COLS = 3 * CONV_WIDTH + 4 * RET_WIDTH

LANES = 128
SUBLANES = 8
VMEM_LIMIT_BYTES = 56 * 1024 * 1024

MIXER_TILE = 512
MLP_TILE = 512
FF_CHUNK = 1024

BF16 = jnp.bfloat16
F32 = jnp.float32


def _dot(a, b):
    return jnp.dot(a, b, preferred_element_type=F32)


def _rms_scale(x):
    return lax.rsqrt(jnp.mean(x * x, axis=-1, keepdims=True) + NORM_EPS)


def _mixer_kernel(cd_ref, x_ref, g1_ref, win_ref, convw_ref, cng_ref, rng_ref, wout_ref,
                  cos_ref, sin_ref, decay_ref, zeta_ref, xi_ref,
                  h_ref, state_ref, pext_ref, mix_ref, lhs_ref, rhs_ref):
    tile = x_ref.shape[0]
    n_chunks = tile // CHUNK
    units = [(c, hd) for c in range(n_chunks) for hd in range(RET_HEADS)]

    @pl.when(pl.program_id(1) == 0)
    def _():
        state_ref[...] = jnp.zeros_like(state_ref)
        pext_ref[0:SUBLANES, :] = jnp.zeros((SUBLANES, CONV_WIDTH), F32)

    x = x_ref[...]
    u = (x * _rms_scale(x) * g1_ref[...]).astype(BF16)

    c0 = 3 * CONV_WIDTH
    zq = _dot(u, win_ref[:, c0:c0 + RET_WIDTH])
    zk = _dot(u, win_ref[:, c0 + RET_WIDTH:c0 + 2 * RET_WIDTH])
    zv = _dot(u, win_ref[:, c0 + 2 * RET_WIDTH:c0 + 3 * RET_WIDTH])
    cos = cos_ref[...]
    sin = sin_ref[...]
    q_rot, k_rot = [], []
    for hd in range(RET_HEADS):
        hs = slice(hd * HEAD_DIM, (hd + 1) * HEAD_DIM)
        qh = zq[:, hs]
        kh = zk[:, hs]
        q_rot.append(qh * cos + pltpu.roll(qh, HEAD_DIM // 2, axis=1) * sin)
        k_rot.append(kh * cos + pltpu.roll(kh, HEAD_DIM // 2, axis=1) * sin)
    v_bf = zv.astype(BF16)

    cb = _dot(u, win_ref[:, 0:CONV_WIDTH])
    cc = _dot(u, win_ref[:, CONV_WIDTH:2 * CONV_WIDTH])
    ch = _dot(u, win_ref[:, 2 * CONV_WIDTH:3 * CONV_WIDTH])

    for n, (c, hd) in enumerate(units):
        rows = slice(c * CHUNK, (c + 1) * CHUNK)
        q = q_rot[hd][rows]
        k = k_rot[hd][rows]
        s = lax.dot_general(q.astype(BF16), k.astype(BF16), (((1,), (1,)), ((), ())),
                            preferred_element_type=F32) * decay_ref[hd]
        lhs_ref[n, :, 0:CHUNK] = s.astype(BF16)
        lhs_ref[n, :, CHUNK:2 * CHUNK] = (q * xi_ref[hd]).astype(BF16)

    states = [state_ref[hd] for hd in range(RET_HEADS)]
    for n, (c, hd) in enumerate(units):
        rows = slice(c * CHUNK, (c + 1) * CHUNK)
        v = v_bf[rows, hd * HEAD_DIM:(hd + 1) * HEAD_DIM]
        kz = (k_rot[hd][rows] * zeta_ref[hd]).astype(BF16)
        kv = lax.dot_general(kz, v, (((0,), (0,)), ((), ())), preferred_element_type=F32)
        rhs_ref[n, 0:CHUNK, :] = v
        rhs_ref[n, CHUNK:2 * CHUNK, :] = states[hd].astype(BF16)
        states[hd] = cd_ref[hd] * states[hd] + kv
    for hd in range(RET_HEADS):
        state_ref[hd] = states[hd]

    p = cc * ch
    pext_ref[SUBLANES:SUBLANES + tile, :] = p
    p1 = pext_ref[SUBLANES - 1:SUBLANES - 1 + tile, :]
    p2 = pext_ref[SUBLANES - 2:SUBLANES - 2 + tile, :]
    y = cb * (p2 * convw_ref[0:1, :] + p1 * convw_ref[1:2, :] + p * convw_ref[2:3, :])
    pext_ref[0:SUBLANES, :] = p[tile - SUBLANES:tile, :]

    lane = lax.broadcasted_iota(jnp.int32, (tile, LANES), 1)
    low = lane < CONV_GROUP_DIM
    for blk in range(CONV_WIDTH // LANES):
        sl = slice(blk * LANES, (blk + 1) * LANES)
        yb = y[:, sl]
        y2 = yb * yb
        ss_lo = jnp.sum(jnp.where(low, y2, 0.0), axis=-1, keepdims=True)
        ss_hi = jnp.sum(jnp.where(low, 0.0, y2), axis=-1, keepdims=True)
        inv = lax.rsqrt(jnp.where(low, ss_lo, ss_hi) * (1.0 / CONV_GROUP_DIM) + NORM_EPS)
        mix_ref[:, sl] = (yb * inv * cng_ref[:, sl]).astype(BF16)

    zg = _dot(u, win_ref[:, c0 + 3 * RET_WIDTH:c0 + 4 * RET_WIDTH])
    acc = _dot(mix_ref[:, 0:CONV_WIDTH], wout_ref[0:CONV_WIDTH, :])

    outs = [_dot(lhs_ref[n], rhs_ref[n]) for n in range(len(units))]
    for hd in range(RET_HEADS):
        hs = slice(hd * HEAD_DIM, (hd + 1) * HEAD_DIM)
        o = jnp.concatenate([outs[c * RET_HEADS + hd] for c in range(n_chunks)], axis=0)
        gate = zg[:, hs]
        gate = gate * (1.0 / (1.0 + jnp.exp(-gate)))
        yr = o * _rms_scale(o) * rng_ref[:, hs] * gate
        mix_ref[:, CONV_WIDTH + hd * HEAD_DIM:CONV_WIDTH + (hd + 1) * HEAD_DIM] = yr.astype(BF16)

    h_ref[...] = x + acc + _dot(mix_ref[:, CONV_WIDTH:], wout_ref[CONV_WIDTH:, :])


def _mlp_kernel(h_ref, g2_ref, wup_ref, wdn_ref, gf_ref, o_ref, hid_ref):
    h = h_ref[...]
    u = (h * _rms_scale(h) * g2_ref[...]).astype(BF16)
    for c in range(D_FF // FF_CHUNK):
        cols = slice(c * FF_CHUNK, (c + 1) * FF_CHUNK)
        a = jnp.maximum(_dot(u, wup_ref[:, cols]), 0.0)
        hid_ref[:, cols] = (a * a).astype(BF16)
    y = h + _dot(hid_ref[...], wdn_ref[...])
    o_ref[...] = y * _rms_scale(y) * gf_ref[...]


def _resident(shape):
    nd = len(shape)
    return pl.BlockSpec(shape, lambda *_: (0,) * nd, pipeline_mode=pl.Buffered(1))


def _retention_tables(seq):
    half = HEAD_DIM // 2
    inv_freq = 1.0 / (ROPE_BASE ** (jnp.arange(half, dtype=F32) / half))
    ang = jnp.arange(seq).astype(F32)[:, None] * inv_freq[None, :]
    cos = jnp.cos(ang)
    sin = jnp.sin(ang)
    cos_t = jnp.concatenate([cos, cos], axis=-1)
    sin_t = jnp.concatenate([-sin, sin], axis=-1)

    log_gamma = jnp.log(1.0 - 2.0 ** (-5.0 - jnp.arange(RET_HEADS, dtype=F32)))
    idx = jnp.arange(CHUNK, dtype=F32)
    diff = idx[:, None] - idx[None, :]
    intra = jnp.where(diff[None] >= 0,
                      jnp.exp(log_gamma[:, None, None] * jnp.maximum(diff, 0.0)[None]), 0.0)
    zeta = jnp.exp(log_gamma[:, None] * (CHUNK - 1 - idx)[None])
    xi = jnp.exp(log_gamma[:, None] * (idx + 1.0)[None])
    chunk_decay = jnp.exp(log_gamma * CHUNK)
    k_scale = HEAD_DIM ** -0.5
    decay_t = intra * k_scale
    zeta_t = jnp.broadcast_to((zeta * k_scale)[:, :, None], (RET_HEADS, CHUNK, HEAD_DIM))
    xi_t = jnp.broadcast_to(xi[:, :, None], (RET_HEADS, CHUNK, HEAD_DIM))
    return cos_t, sin_t, decay_t, zeta_t, xi_t, chunk_decay


def kernel(x, norm1_g, w_in, conv_w, conv_norm_g, ret_norm_g, w_out, norm2_g, w_up, w_down, final_norm_g):
    batch, seq, d_model = x.shape
    assert d_model == D_MODEL and w_in.shape == (D_MODEL, IN_COLS)
    assert seq % MIXER_TILE == 0 and MIXER_TILE % CHUNK == 0 and (batch * seq) % MLP_TILE == 0

    cos_t, sin_t, decay_t, zeta_t, xi_t, chunk_decay = _retention_tables(seq)
    row = lambda g: g.reshape(1, -1).astype(F32)
    n_units = (MIXER_TILE // CHUNK) * RET_HEADS

    tile_spec = pl.BlockSpec((None, MIXER_TILE, D_MODEL), lambda b, j: (b, j, 0))
    rope_spec = pl.BlockSpec((MIXER_TILE, HEAD_DIM), lambda b, j: (j, 0))
    h = pl.pallas_call(
        _mixer_kernel,
        grid=(batch, seq // MIXER_TILE),
        in_specs=[
            pl.BlockSpec(memory_space=pltpu.SMEM),
            tile_spec,
            _resident((1, D_MODEL)),
            _resident((D_MODEL, IN_COLS)),
            _resident((CONV_K, CONV_WIDTH)),
            _resident((1, CONV_WIDTH)),
            _resident((1, RET_WIDTH)),
            _resident((D_MODEL, D_MODEL)),
            rope_spec,
            rope_spec,
            _resident((RET_HEADS, CHUNK, CHUNK)),
            _resident((RET_HEADS, CHUNK, HEAD_DIM)),
            _resident((RET_HEADS, CHUNK, HEAD_DIM)),
        ],
        out_specs=tile_spec,
        out_shape=jax.ShapeDtypeStruct(x.shape, F32),
        scratch_shapes=[
            pltpu.VMEM((RET_HEADS, HEAD_DIM, HEAD_DIM), F32),
            pltpu.VMEM((MIXER_TILE + SUBLANES, CONV_WIDTH), F32),
            pltpu.VMEM((MIXER_TILE, D_MODEL), BF16),
            pltpu.VMEM((n_units, CHUNK, 2 * CHUNK), BF16),
            pltpu.VMEM((n_units, 2 * CHUNK, HEAD_DIM), BF16),
        ],
        compiler_params=pltpu.CompilerParams(
            dimension_semantics=("arbitrary", "arbitrary"),
            vmem_limit_bytes=VMEM_LIMIT_BYTES),
        name="mixer",
    )(chunk_decay, x, row(norm1_g), w_in.astype(BF16), conv_w, row(conv_norm_g), row(ret_norm_g),
      w_out.astype(BF16), cos_t, sin_t, decay_t, zeta_t, xi_t)

    tokens = batch * seq
    tok_spec = pl.BlockSpec((MLP_TILE, D_MODEL), lambda i: (i, 0))
    out = pl.pallas_call(
        _mlp_kernel,
        grid=(tokens // MLP_TILE,),
        in_specs=[
            tok_spec,
            _resident((1, D_MODEL)),
            _resident((D_MODEL, D_FF)),
            _resident((D_FF, D_MODEL)),
            _resident((1, D_MODEL)),
        ],
        out_specs=tok_spec,
        out_shape=jax.ShapeDtypeStruct((tokens, D_MODEL), F32),
        scratch_shapes=[pltpu.VMEM((MLP_TILE, D_FF), BF16)],
        compiler_params=pltpu.CompilerParams(
            dimension_semantics=("arbitrary",),
            vmem_limit_bytes=VMEM_LIMIT_BYTES),
        name="mlp",
    )(h.reshape(tokens, D_MODEL), row(norm2_g), w_up.astype(BF16), w_down.astype(BF16), row(final_norm_g))
    return out.reshape(batch, seq, D_MODEL)
```

```python
import functools

import jax
import jax.numpy as jnp
import numpy as np
from jax import lax
from jax.experimental import pallas as pl
from jax.experimental.pallas import tpu as pltpu

D_MODEL = 1024
CONV_WIDTH = 512
CONV_GROUPS = 8
CONV_GROUP_DIM = CONV_WIDTH // CONV_GROUPS
CONV_K = 3
RET_WIDTH = 512
RET_HEADS = 4
HEAD_DIM = RET_WIDTH // RET_HEADS
CHUNK = 128
ROPE_BASE = 10000.0
D_FF = 4 * D_MODEL
NORM_EPS = 1e-6
IN_COLS = 3 * CONV_WIDTH + 4 * RET_WIDTH

LANES = 128
SUBLANES = 8
VMEM_LIMIT_BYTES = 56 * 1024 * 1024

MIXER_TILE = 512
MLP_TILE = 512
FF_CHUNK = 1024

BF16 = jnp.bfloat16
F32 = jnp.float32


def _dot(a, b):
    return jnp.dot(a, b, preferred_element_type=F32)


def _rms_scale(x):
    return lax.rsqrt(jnp.mean(x * x, axis=-1, keepdims=True) + NORM_EPS)


def _mixer_kernel(cd_ref, x_ref, g1_ref, win_ref, convw_ref, cng_ref, rng_ref, wout_ref,
                  cos_ref, sin_ref, decay_ref, zeta_ref, xi_ref, wup_ref, wdn_ref,
                  h_ref, wup_bf_ref, wdn_bf_ref, state_ref, pext_ref, mix_ref, lhs_ref, rhs_ref):
    tile = x_ref.shape[0]
    wup_bf_ref[...] = wup_ref[...].astype(BF16)
    wdn_bf_ref[...] = wdn_ref[...].astype(BF16)
    n_chunks = tile // CHUNK
    units = [(c, hd) for c in range(n_chunks) for hd in range(RET_HEADS)]

    @pl.when(pl.program_id(1) == 0)
    def _():
        state_ref[...] = jnp.zeros_like(state_ref)
        pext_ref[0:SUBLANES, :] = jnp.zeros((SUBLANES, CONV_WIDTH), F32)

    x = x_ref[...]
    u = (x * _rms_scale(x) * g1_ref[...]).astype(BF16)

    c0 = 3 * CONV_WIDTH
    zq = _dot(u, win_ref[:, c0:c0 + RET_WIDTH])
    zk = _dot(u, win_ref[:, c0 + RET_WIDTH:c0 + 2 * RET_WIDTH])
    zv = _dot(u, win_ref[:, c0 + 2 * RET_WIDTH:c0 + 3 * RET_WIDTH])
    cos = cos_ref[...]
    sin = sin_ref[...]
    q_rot, k_rot = [], []
    for hd in range(RET_HEADS):
        hs = slice(hd * HEAD_DIM, (hd + 1) * HEAD_DIM)
        qh = zq[:, hs]
        kh = zk[:, hs]
        q_rot.append(qh * cos + pltpu.roll(qh, HEAD_DIM // 2, axis=1) * sin)
        k_rot.append(kh * cos + pltpu.roll(kh, HEAD_DIM // 2, axis=1) * sin)
    v_bf = zv.astype(BF16)

    cb = _dot(u, win_ref[:, 0:CONV_WIDTH])
    cc = _dot(u, win_ref[:, CONV_WIDTH:2 * CONV_WIDTH])
    ch = _dot(u, win_ref[:, 2 * CONV_WIDTH:3 * CONV_WIDTH])

    for n, (c, hd) in enumerate(units):
        rows = slice(c * CHUNK, (c + 1) * CHUNK)
        q = q_rot[hd][rows]
        k = k_rot[hd][rows]
        s = lax.dot_general(q.astype(BF16), k.astype(BF16), (((1,), (1,)), ((), ())),
                            preferred_element_type=F32) * decay_ref[hd]
        lhs_ref[n, :, 0:CHUNK] = s.astype(BF16)
        lhs_ref[n, :, CHUNK:2 * CHUNK] = (q * xi_ref[hd]).astype(BF16)

    states = [state_ref[hd] for hd in range(RET_HEADS)]
    for n, (c, hd) in enumerate(units):
        rows = slice(c * CHUNK, (c + 1) * CHUNK)
        v = v_bf[rows, hd * HEAD_DIM:(hd + 1) * HEAD_DIM]
        kz = (k_rot[hd][rows] * zeta_ref[hd]).astype(BF16)
        kv = lax.dot_general(kz, v, (((0,), (0,)), ((), ())), preferred_element_type=F32)
        rhs_ref[n, 0:CHUNK, :] = v
        rhs_ref[n, CHUNK:2 * CHUNK, :] = states[hd].astype(BF16)
        states[hd] = cd_ref[hd] * states[hd] + kv
    for hd in range(RET_HEADS):
        state_ref[hd] = states[hd]

    p = cc * ch
    pext_ref[SUBLANES:SUBLANES + tile, :] = p
    p1 = pext_ref[SUBLANES - 1:SUBLANES - 1 + tile, :]
    p2 = pext_ref[SUBLANES - 2:SUBLANES - 2 + tile, :]
    y = cb * (p2 * convw_ref[0:1, :] + p1 * convw_ref[1:2, :] + p * convw_ref[2:3, :])
    pext_ref[0:SUBLANES, :] = p[tile - SUBLANES:tile, :]

    lane = lax.broadcasted_iota(jnp.int32, (tile, LANES), 1)
    low = lane < CONV_GROUP_DIM
    for blk in range(CONV_WIDTH // LANES):
        sl = slice(blk * LANES, (blk + 1) * LANES)
        yb = y[:, sl]
        y2 = yb * yb
        ss_lo = jnp.sum(jnp.where(low, y2, 0.0), axis=-1, keepdims=True)
        ss_hi = jnp.sum(jnp.where(low, 0.0, y2), axis=-1, keepdims=True)
        inv = lax.rsqrt(jnp.where(low, ss_lo, ss_hi) * (1.0 / CONV_GROUP_DIM) + NORM_EPS)
        mix_ref[:, sl] = (yb * inv * cng_ref[:, sl]).astype(BF16)

    zg = _dot(u, win_ref[:, c0 + 3 * RET_WIDTH:c0 + 4 * RET_WIDTH])
    acc = _dot(mix_ref[:, 0:CONV_WIDTH], wout_ref[0:CONV_WIDTH, :])

    outs = [_dot(lhs_ref[n], rhs_ref[n]) for n in range(len(units))]
    for hd in range(RET_HEADS):
        hs = slice(hd * HEAD_DIM, (hd + 1) * HEAD_DIM)
        o = jnp.concatenate([outs[c * RET_HEADS + hd] for c in range(n_chunks)], axis=0)
        gate = zg[:, hs]
        gate = gate * (1.0 / (1.0 + jnp.exp(-gate)))
        yr = o * _rms_scale(o) * rng_ref[:, hs] * gate
        mix_ref[:, CONV_WIDTH + hd * HEAD_DIM:CONV_WIDTH + (hd + 1) * HEAD_DIM] = yr.astype(BF16)

    h_ref[...] = x + acc + _dot(mix_ref[:, CONV_WIDTH:], wout_ref[CONV_WIDTH:, :])


def _mlp_kernel(h_ref, g2_ref, wup_ref, wdn_ref, gf_ref, o_ref, hid_ref):
    h = h_ref[...]
    u = (h * _rms_scale(h) * g2_ref[...]).astype(BF16)
    for c in range(D_FF // FF_CHUNK):
        cols = slice(c * FF_CHUNK, (c + 1) * FF_CHUNK)
        a = jnp.maximum(_dot(u, wup_ref[:, cols]), 0.0)
        hid_ref[:, cols] = (a * a).astype(BF16)
    y = h + _dot(hid_ref[...], wdn_ref[...])
    o_ref[...] = y * _rms_scale(y) * gf_ref[...]


def _resident(shape):
    nd = len(shape)
    return pl.BlockSpec(shape, lambda *_: (0,) * nd, pipeline_mode=pl.Buffered(1))


@functools.lru_cache(maxsize=None)
def _retention_tables(seq):
    half = HEAD_DIM // 2
    inv_freq = 1.0 / (ROPE_BASE ** (np.arange(half, dtype=np.float64) / half))
    ang = np.arange(seq, dtype=np.float64)[:, None] * inv_freq[None, :]
    cos = np.cos(ang)
    sin = np.sin(ang)
    cos_t = np.concatenate([cos, cos], axis=-1)
    sin_t = np.concatenate([-sin, sin], axis=-1)

    log_gamma = np.log(1.0 - 2.0 ** (-5.0 - np.arange(RET_HEADS, dtype=np.float64)))
    idx = np.arange(CHUNK, dtype=np.float64)
    diff = idx[:, None] - idx[None, :]
    intra = np.where(diff[None] >= 0, np.exp(log_gamma[:, None, None] * np.maximum(diff, 0.0)[None]), 0.0)
    zeta = np.exp(log_gamma[:, None] * (CHUNK - 1 - idx)[None])
    xi = np.exp(log_gamma[:, None] * (idx + 1.0)[None])
    chunk_decay = np.exp(log_gamma * CHUNK)
    k_scale = HEAD_DIM ** -0.5
    decay_t = intra * k_scale
    zeta_t = np.broadcast_to((zeta * k_scale)[:, :, None], (RET_HEADS, CHUNK, HEAD_DIM))
    xi_t = np.broadcast_to(xi[:, :, None], (RET_HEADS, CHUNK, HEAD_DIM))
    return tuple(np.ascontiguousarray(t, dtype=np.float32)
                 for t in (cos_t, sin_t, decay_t, zeta_t, xi_t, chunk_decay))


def kernel(x, norm1_g, w_in, conv_w, conv_norm_g, ret_norm_g, w_out, norm2_g, w_up, w_down, final_norm_g):
    batch, seq, d_model = x.shape
    assert d_model == D_MODEL and w_in.shape == (D_MODEL, IN_COLS)
    assert seq % MIXER_TILE == 0 and MIXER_TILE % CHUNK == 0 and (batch * seq) % MLP_TILE == 0

    cos_t, sin_t, decay_t, zeta_t, xi_t, chunk_decay = _retention_tables(seq)
    row = lambda g: g.reshape(1, -1).astype(F32)
    n_units = (MIXER_TILE // CHUNK) * RET_HEADS

    tile_spec = pl.BlockSpec((None, MIXER_TILE, D_MODEL), lambda b, j: (b, j, 0))
    rope_spec = pl.BlockSpec((MIXER_TILE, HEAD_DIM), lambda b, j: (j, 0))
    seq_tiles = seq // MIXER_TILE
    n_steps = batch * seq_tiles
    assert D_MODEL % n_steps == 0 and D_FF % n_steps == 0
    wup_slab = pl.BlockSpec((D_MODEL // n_steps, D_FF), lambda b, j: (b * seq_tiles + j, 0))
    wdn_slab = pl.BlockSpec((D_FF // n_steps, D_MODEL), lambda b, j: (b * seq_tiles + j, 0))
    h, w_up_bf, w_down_bf = pl.pallas_call(
        _mixer_kernel,
        grid=(batch, seq_tiles),
        in_specs=[
            pl.BlockSpec(memory_space=pltpu.SMEM),
            tile_spec,
            _resident((1, D_MODEL)),
            _resident((D_MODEL, IN_COLS)),
            _resident((CONV_K, CONV_WIDTH)),
            _resident((1, CONV_WIDTH)),
            _resident((1, RET_WIDTH)),
            _resident((D_MODEL, D_MODEL)),
            rope_spec,
            rope_spec,
            _resident((RET_HEADS, CHUNK, CHUNK)),
            _resident((RET_HEADS, CHUNK, HEAD_DIM)),
            _resident((RET_HEADS, CHUNK, HEAD_DIM)),
            wup_slab,
            wdn_slab,
        ],
        out_specs=[tile_spec, wup_slab, wdn_slab],
        out_shape=[jax.ShapeDtypeStruct(x.shape, F32),
                   jax.ShapeDtypeStruct(w_up.shape, BF16),
                   jax.ShapeDtypeStruct(w_down.shape, BF16)],
        scratch_shapes=[
            pltpu.VMEM((RET_HEADS, HEAD_DIM, HEAD_DIM), F32),
            pltpu.VMEM((MIXER_TILE + SUBLANES, CONV_WIDTH), F32),
            pltpu.VMEM((MIXER_TILE, D_MODEL), BF16),
            pltpu.VMEM((n_units, CHUNK, 2 * CHUNK), BF16),
            pltpu.VMEM((n_units, 2 * CHUNK, HEAD_DIM), BF16),
        ],
        compiler_params=pltpu.CompilerParams(
            dimension_semantics=("arbitrary", "arbitrary"),
            vmem_limit_bytes=VMEM_LIMIT_BYTES),
        name="mixer",
    )(chunk_decay, x, row(norm1_g), w_in.astype(BF16), conv_w, row(conv_norm_g), row(ret_norm_g),
      w_out.astype(BF16), cos_t, sin_t, decay_t, zeta_t, xi_t, w_up, w_down)

    tokens = batch * seq
    tok_spec = pl.BlockSpec((MLP_TILE, D_MODEL), lambda i: (i, 0))
    out = pl.pallas_call(
        _mlp_kernel,
        grid=(tokens // MLP_TILE,),
        in_specs=[
            tok_spec,
            _resident((1, D_MODEL)),
            _resident((D_MODEL, D_FF)),
            _resident((D_FF, D_MODEL)),
            _resident((1, D_MODEL)),
        ],
        out_specs=tok_spec,
        out_shape=jax.ShapeDtypeStruct((tokens, D_MODEL), F32),
        scratch_shapes=[pltpu.VMEM((MLP_TILE, D_FF), BF16)],
        compiler_params=pltpu.CompilerParams(
            dimension_semantics=("arbitrary",),
            vmem_limit_bytes=VMEM_LIMIT_BYTES),
        name="mlp",
    )(h.reshape(tokens, D_MODEL), row(norm2_g), w_up_bf, w_down_bf, row(final_norm_g))
    return out.reshape(batch, seq, D_MODEL)
```

```python
import functools

import jax
import jax.numpy as jnp
import numpy as np
from jax import lax
from jax.experimental import pallas as pl
from jax.experimental.pallas import tpu as pltpu

D_MODEL = 1024
CONV_WIDTH = 512
CONV_GROUPS = 8
CONV_GROUP_DIM = CONV_WIDTH // CONV_GROUPS
CONV_K = 3
RET_WIDTH = 512
RET_HEADS = 4
HEAD_DIM = RET_WIDTH // RET_HEADS
CHUNK = 128
ROPE_BASE = 10000.0
D_FF = 4 * D_MODEL
NORM_EPS = 1e-6
IN_COLS = 3 * CONV_WIDTH + 4 * RET_WIDTH

LANES = 128
SUBLANES = 8
VMEM_LIMIT_BYTES = 56 * 1024 * 1024

MIXER_TILE = 1024
MLP_TILE = 1024
MLP_SUB = 256
FF_CHUNK = 1024

BF16 = jnp.bfloat16
F32 = jnp.float32


def _dot(a, b):
    return jnp.dot(a, b, preferred_element_type=F32)


def _rms_scale(x):
    return lax.rsqrt(jnp.mean(x * x, axis=-1, keepdims=True) + NORM_EPS)


def _mixer_kernel(cd_ref, x_ref, g1_ref, win_ref, convw_ref, cng_ref, rng_ref, wout_ref,
                  cos_ref, sin_ref, decay_ref, zeta_ref, xi_ref, wup_ref, wdn_ref,
                  h_ref, wup_bf_ref, wdn_bf_ref, state_ref, pext_ref, mix_ref, lhs_ref, rhs_ref):
    tile = x_ref.shape[0]
    wup_bf_ref[...] = wup_ref[...].astype(BF16)
    wdn_bf_ref[...] = wdn_ref[...].astype(BF16)
    n_chunks = tile // CHUNK
    units = [(c, hd) for c in range(n_chunks) for hd in range(RET_HEADS)]

    @pl.when(pl.program_id(1) == 0)
    def _():
        state_ref[...] = jnp.zeros_like(state_ref)
        pext_ref[0:SUBLANES, :] = jnp.zeros((SUBLANES, CONV_WIDTH), F32)

    x = x_ref[...]
    u = (x * _rms_scale(x) * g1_ref[...]).astype(BF16)

    c0 = 3 * CONV_WIDTH

    zq = jnp.dot(u, win_ref[:, c0:c0 + RET_WIDTH], preferred_element_type=F32)
    zk = jnp.dot(u, win_ref[:, c0 + RET_WIDTH:c0 + 2 * RET_WIDTH], preferred_element_type=F32)
    cb = jnp.dot(u, win_ref[:, 0:CONV_WIDTH], preferred_element_type=F32)
    cc = jnp.dot(u, win_ref[:, CONV_WIDTH:2 * CONV_WIDTH], preferred_element_type=F32)

    cos = cos_ref[...]
    sin = sin_ref[...]
    q_rot, k_rot = [], []
    for hd in range(RET_HEADS):
        hs = slice(hd * HEAD_DIM, (hd + 1) * HEAD_DIM)
        qh = zq[:, hs]
        kh = zk[:, hs]
        q_rot.append(qh * cos + pltpu.roll(qh, HEAD_DIM // 2, axis=1) * sin)
        k_rot.append(kh * cos + pltpu.roll(kh, HEAD_DIM // 2, axis=1) * sin)

    for n, (c, hd) in enumerate(units):
        rows = slice(c * CHUNK, (c + 1) * CHUNK)
        q = q_rot[hd][rows]
        k = k_rot[hd][rows]
        s = lax.dot_general(q.astype(BF16), k.astype(BF16), (((1,), (1,)), ((), ())),
                            preferred_element_type=F32) * decay_ref[hd]
        lhs_ref[n, :, 0:CHUNK] = s.astype(BF16)
        lhs_ref[n, :, CHUNK:2 * CHUNK] = (q * xi_ref[hd]).astype(BF16)

    ch = jnp.dot(u, win_ref[:, 2 * CONV_WIDTH:3 * CONV_WIDTH], preferred_element_type=F32)
    zv = jnp.dot(u, win_ref[:, c0 + 2 * RET_WIDTH:c0 + 3 * RET_WIDTH], preferred_element_type=F32)
    v_bf = zv.astype(BF16)

    states = [state_ref[hd] for hd in range(RET_HEADS)]
    for n, (c, hd) in enumerate(units):
        rows = slice(c * CHUNK, (c + 1) * CHUNK)
        v = v_bf[rows, hd * HEAD_DIM:(hd + 1) * HEAD_DIM]
        kz = (k_rot[hd][rows] * zeta_ref[hd]).astype(BF16)
        kv = lax.dot_general(kz, v, (((0,), (0,)), ((), ())), preferred_element_type=F32)
        rhs_ref[n, 0:CHUNK, :] = v
        rhs_ref[n, CHUNK:2 * CHUNK, :] = states[hd].astype(BF16)
        states[hd] = cd_ref[hd] * states[hd] + kv
    for hd in range(RET_HEADS):
        state_ref[hd] = states[hd]

    zg = jnp.dot(u, win_ref[:, c0 + 3 * RET_WIDTH:c0 + 4 * RET_WIDTH], preferred_element_type=F32)

    p = cc * ch
    pext_ref[SUBLANES:SUBLANES + tile, :] = p
    p1 = pext_ref[SUBLANES - 1:SUBLANES - 1 + tile, :]
    p2 = pext_ref[SUBLANES - 2:SUBLANES - 2 + tile, :]
    y = cb * (p2 * convw_ref[0:1, :] + p1 * convw_ref[1:2, :] + p * convw_ref[2:3, :])
    pext_ref[0:SUBLANES, :] = p[tile - SUBLANES:tile, :]

    lane = lax.broadcasted_iota(jnp.int32, (tile, LANES), 1)
    low = lane < CONV_GROUP_DIM
    for blk in range(CONV_WIDTH // LANES):
        sl = slice(blk * LANES, (blk + 1) * LANES)
        yb = y[:, sl]
        y2 = yb * yb
        ss_lo = jnp.sum(jnp.where(low, y2, 0.0), axis=-1, keepdims=True)
        ss_hi = jnp.sum(jnp.where(low, 0.0, y2), axis=-1, keepdims=True)
        inv = lax.rsqrt(jnp.where(low, ss_lo, ss_hi) * (1.0 / CONV_GROUP_DIM) + NORM_EPS)
        mix_ref[:, sl] = (yb * inv * cng_ref[:, sl]).astype(BF16)

    outs = [jnp.dot(lhs_ref[n], rhs_ref[n], preferred_element_type=F32) for n in range(len(units))]

    acc = jnp.dot(mix_ref[:, 0:CONV_WIDTH], wout_ref[0:CONV_WIDTH, :], preferred_element_type=F32)

    for hd in range(RET_HEADS):
        hs = slice(hd * HEAD_DIM, (hd + 1) * HEAD_DIM)
        o = jnp.concatenate([outs[c * RET_HEADS + hd] for c in range(n_chunks)], axis=0)
        gate = zg[:, hs]
        gate = gate * (1.0 / (1.0 + jnp.exp(-gate)))
        yr = o * _rms_scale(o) * rng_ref[:, hs] * gate
        mix_ref[:, CONV_WIDTH + hd * HEAD_DIM:CONV_WIDTH + (hd + 1) * HEAD_DIM] = yr.astype(BF16)

    h_ref[...] = x + acc + jnp.dot(mix_ref[:, CONV_WIDTH:], wout_ref[CONV_WIDTH:, :], preferred_element_type=F32)


def _mlp_kernel(h_ref, g2_ref, wup_ref, wdn_ref, gf_ref, o_ref, hid_ref):
    for r in range(h_ref.shape[0] // MLP_SUB):
        rows = slice(r * MLP_SUB, (r + 1) * MLP_SUB)
        h = h_ref[rows, :]
        u = (h * _rms_scale(h) * g2_ref[...]).astype(BF16)
        for c in range(D_FF // FF_CHUNK):
            cols = slice(c * FF_CHUNK, (c + 1) * FF_CHUNK)
            a = jnp.maximum(_dot(u, wup_ref[:, cols]), 0.0)
            hid_ref[rows, cols] = (a * a).astype(BF16)
        y = h + _dot(hid_ref[rows, :], wdn_ref[...])
        o_ref[rows, :] = y * _rms_scale(y) * gf_ref[...]


def _resident(shape):
    nd = len(shape)
    return pl.BlockSpec(shape, lambda *_: (0,) * nd, pipeline_mode=pl.Buffered(1))


@functools.lru_cache(maxsize=None)
def _retention_tables(seq):
    half = HEAD_DIM // 2
    inv_freq = 1.0 / (ROPE_BASE ** (np.arange(half, dtype=np.float64) / half))
    ang = np.arange(seq, dtype=np.float64)[:, None] * inv_freq[None, :]
    cos = np.cos(ang)
    sin = np.sin(ang)
    cos_t = np.concatenate([cos, cos], axis=-1)
    sin_t = np.concatenate([-sin, sin], axis=-1)

    log_gamma = np.log(1.0 - 2.0 ** (-5.0 - np.arange(RET_HEADS, dtype=np.float64)))
    idx = np.arange(CHUNK, dtype=np.float64)
    diff = idx[:, None] - idx[None, :]
    intra = np.where(diff[None] >= 0, np.exp(log_gamma[:, None, None] * np.maximum(diff, 0.0)[None]), 0.0)
    zeta = np.exp(log_gamma[:, None] * (CHUNK - 1 - idx)[None])
    xi = np.exp(log_gamma[:, None] * (idx + 1.0)[None])
    chunk_decay = np.exp(log_gamma * CHUNK)
    k_scale = HEAD_DIM ** -0.5
    decay_t = intra * k_scale
    zeta_t = np.broadcast_to((zeta * k_scale)[:, :, None], (RET_HEADS, CHUNK, HEAD_DIM))
    xi_t = np.broadcast_to(xi[:, :, None], (RET_HEADS, CHUNK, HEAD_DIM))
    return tuple(np.ascontiguousarray(t, dtype=np.float32)
                 for t in (cos_t, sin_t, decay_t, zeta_t, xi_t, chunk_decay))


def kernel(x, norm1_g, w_in, conv_w, conv_norm_g, ret_norm_g, w_out, norm2_g, w_up, w_down, final_norm_g):
    batch, seq, d_model = x.shape
    assert d_model == D_MODEL and w_in.shape == (D_MODEL, IN_COLS)
    assert seq % MIXER_TILE == 0 and MIXER_TILE % CHUNK == 0 and (batch * seq) % MLP_TILE == 0

    cos_t, sin_t, decay_t, zeta_t, xi_t, chunk_decay = _retention_tables(seq)
    row = lambda g: g.reshape(1, -1).astype(F32)
    n_units = (MIXER_TILE // CHUNK) * RET_HEADS

    tile_spec = pl.BlockSpec((None, MIXER_TILE, D_MODEL), lambda b, j: (b, j, 0))
    rope_spec = pl.BlockSpec((MIXER_TILE, HEAD_DIM), lambda b, j: (j, 0))
    seq_tiles = seq // MIXER_TILE
    n_steps = batch * seq_tiles
    assert D_MODEL % n_steps == 0 and D_FF % n_steps == 0
    wup_slab = pl.BlockSpec((D_MODEL // n_steps, D_FF), lambda b, j: (b * seq_tiles + j, 0))
    wdn_slab = pl.BlockSpec((D_FF // n_steps, D_MODEL), lambda b, j: (b * seq_tiles + j, 0))
    h, w_up_bf, w_down_bf = pl.pallas_call(
        _mixer_kernel,
        grid=(batch, seq_tiles),
        in_specs=[
            pl.BlockSpec(memory_space=pltpu.SMEM),
            tile_spec,
            _resident((1, D_MODEL)),
            _resident((D_MODEL, IN_COLS)),
            _resident((CONV_K, CONV_WIDTH)),
            _resident((1, CONV_WIDTH)),
            _resident((1, RET_WIDTH)),
            _resident((D_MODEL, D_MODEL)),
            rope_spec,
            rope_spec,
            _resident((RET_HEADS, CHUNK, CHUNK)),
            _resident((RET_HEADS, CHUNK, HEAD_DIM)),
            _resident((RET_HEADS, CHUNK, HEAD_DIM)),
            wup_slab,
            wdn_slab,
        ],
        out_specs=[tile_spec, wup_slab, wdn_slab],
        out_shape=[jax.ShapeDtypeStruct(x.shape, F32),
                   jax.ShapeDtypeStruct(w_up.shape, BF16),
                   jax.ShapeDtypeStruct(w_down.shape, BF16)],
        scratch_shapes=[
            pltpu.VMEM((RET_HEADS, HEAD_DIM, HEAD_DIM), F32),
            pltpu.VMEM((MIXER_TILE + SUBLANES, CONV_WIDTH), F32),
            pltpu.VMEM((MIXER_TILE, D_MODEL), BF16),
            pltpu.VMEM((n_units, CHUNK, 2 * CHUNK), BF16),
            pltpu.VMEM((n_units, 2 * CHUNK, HEAD_DIM), BF16),
        ],
        compiler_params=pltpu.CompilerParams(
            dimension_semantics=("arbitrary", "arbitrary"),
            vmem_limit_bytes=VMEM_LIMIT_BYTES),
        name="mixer",
    )(chunk_decay, x, row(norm1_g), w_in.astype(BF16), conv_w, row(conv_norm_g), row(ret_norm_g),
      w_out.astype(BF16), cos_t, sin_t, decay_t, zeta_t, xi_t, w_up, w_down)

    tokens = batch * seq
    tok_spec = pl.BlockSpec((MLP_TILE, D_MODEL), lambda i: (i, 0))
    out = pl.pallas_call(
        _mlp_kernel,
        grid=(tokens // MLP_TILE,),
        in_specs=[
            tok_spec,
            _resident((1, D_MODEL)),
            _resident((D_MODEL, D_FF)),
            _resident((D_FF, D_MODEL)),
            _resident((1, D_MODEL)),
        ],
        out_specs=tok_spec,
        out_shape=jax.ShapeDtypeStruct((tokens, D_MODEL), F32),
        scratch_shapes=[pltpu.VMEM((MLP_TILE, D_FF), BF16)],
        compiler_params=pltpu.CompilerParams(
            dimension_semantics=("arbitrary",),
            vmem_limit_bytes=VMEM_LIMIT_BYTES),
        name="mlp",
    )(h.reshape(tokens, D_MODEL), row(norm2_g), w_up_bf, w_down_bf, row(final_norm_g))
    return out.reshape(batch, seq, D_MODEL)
```

```python
import functools

import jax
import jax.numpy as jnp
import numpy as np
from jax import lax
from jax.experimental import pallas as pl
from jax.experimental.pallas import tpu as pltpu

D_MODEL = 1024
CONV_WIDTH = 512
CONV_GROUPS = 8
CONV_GROUP_DIM = CONV_WIDTH // CONV_GROUPS
CONV_K = 3
RET_WIDTH = 512
RET_HEADS = 4
HEAD_DIM = RET_WIDTH // RET_HEADS
CHUNK = 128
ROPE_BASE = 10000.0
D_FF = 4 * D_MODEL
NORM_EPS = 1e-6
IN_COLS = 3 * CONV_WIDTH + 4 * RET_WIDTH

LANES = 128
SUBLANES = 8
VMEM_LIMIT_BYTES = 56 * 1024 * 1024

MIXER_TILE = 1024
MLP_TILE = 1024
MLP_SUB = 256
FF_CHUNK = 1024

BF16 = jnp.bfloat16
F32 = jnp.float32


def _dot(a, b):
    return jnp.dot(a, b, preferred_element_type=F32)


def _rms_scale(x):
    return lax.rsqrt(jnp.mean(x * x, axis=-1, keepdims=True) + NORM_EPS)


def _mixer_kernel(cd_ref, x_ref, g1_ref, win_ref, convw_ref, cng_ref, rng_ref, wout_ref,
                  cos_ref, sin_ref, decay_ref, zeta_ref, xi_ref, wup_ref, wdn_ref,
                  h_ref, wup_bf_ref, wdn_bf_ref, state_ref, pd1_ref, pd2_ref, mix_ref, lhs_ref, rhs_ref):
    tile = x_ref.shape[0]
    wup_bf_ref[...] = wup_ref[...].astype(BF16)
    wdn_bf_ref[...] = wdn_ref[...].astype(BF16)
    n_chunks = tile // CHUNK
    units = [(c, hd) for c in range(n_chunks) for hd in range(RET_HEADS)]

    @pl.when(pl.program_id(1) == 0)
    def _():
        state_ref[...] = jnp.zeros_like(state_ref)
        zeros = jnp.zeros((SUBLANES, CONV_WIDTH), F32)
        pd1_ref[0:SUBLANES, :] = zeros
        pd2_ref[0:SUBLANES, :] = zeros
        pd1_ref[tile:tile + SUBLANES, :] = zeros
        pd2_ref[tile:tile + SUBLANES, :] = zeros

    x = x_ref[...]
    u = (x * _rms_scale(x) * g1_ref[...]).astype(BF16)

    c0 = 3 * CONV_WIDTH

    zq = jnp.dot(u, win_ref[:, c0:c0 + RET_WIDTH], preferred_element_type=F32)
    zk = jnp.dot(u, win_ref[:, c0 + RET_WIDTH:c0 + 2 * RET_WIDTH], preferred_element_type=F32)
    cb = jnp.dot(u, win_ref[:, 0:CONV_WIDTH], preferred_element_type=F32)
    cc = jnp.dot(u, win_ref[:, CONV_WIDTH:2 * CONV_WIDTH], preferred_element_type=F32)

    cos = cos_ref[...]
    sin = sin_ref[...]
    q_rot, kt_rot = [], []
    for hd in range(RET_HEADS):
        hs = slice(hd * HEAD_DIM, (hd + 1) * HEAD_DIM)
        qh = zq[:, hs]
        kh = zk[:, hs]
        q_rot.append(qh * cos + pltpu.roll(qh, HEAD_DIM // 2, axis=1) * sin)
        kt_rot.append((kh * cos + pltpu.roll(kh, HEAD_DIM // 2, axis=1) * sin).T)

    for n, (c, hd) in enumerate(units):
        rows = slice(c * CHUNK, (c + 1) * CHUNK)
        q = q_rot[hd][rows]
        kt = kt_rot[hd][:, rows]
        s = jnp.dot(q.astype(BF16), kt.astype(BF16), preferred_element_type=F32) * decay_ref[hd]
        lhs_ref[n, :, 0:CHUNK] = s.astype(BF16)
        lhs_ref[n, :, CHUNK:2 * CHUNK] = (q * xi_ref[hd]).astype(BF16)

    ch = jnp.dot(u, win_ref[:, 2 * CONV_WIDTH:3 * CONV_WIDTH], preferred_element_type=F32)
    zv = jnp.dot(u, win_ref[:, c0 + 2 * RET_WIDTH:c0 + 3 * RET_WIDTH], preferred_element_type=F32)
    v_bf = zv.astype(BF16)

    states = [state_ref[hd] for hd in range(RET_HEADS)]
    for n, (c, hd) in enumerate(units):
        rows = slice(c * CHUNK, (c + 1) * CHUNK)
        v = v_bf[rows, hd * HEAD_DIM:(hd + 1) * HEAD_DIM]
        kzt = (kt_rot[hd][:, rows] * zeta_ref[hd]).astype(BF16)
        kv = jnp.dot(kzt, v, preferred_element_type=F32)
        rhs_ref[n, 0:CHUNK, :] = v
        rhs_ref[n, CHUNK:2 * CHUNK, :] = states[hd].astype(BF16)
        states[hd] = cd_ref[hd] * states[hd] + kv
    for hd in range(RET_HEADS):
        state_ref[hd] = states[hd]

    zg = jnp.dot(u, win_ref[:, c0 + 3 * RET_WIDTH:c0 + 4 * RET_WIDTH], preferred_element_type=F32)

    p = cc * ch
    pd1_ref[1:1 + tile, :] = p
    pd2_ref[2:2 + tile, :] = p
    p1 = pd1_ref[0:tile, :]
    p2 = pd2_ref[0:tile, :]
    y = cb * (p2 * convw_ref[0:1, :] + p1 * convw_ref[1:2, :] + p * convw_ref[2:3, :])
    pd1_ref[0:SUBLANES, :] = pd1_ref[tile:tile + SUBLANES, :]
    pd2_ref[0:SUBLANES, :] = pd2_ref[tile:tile + SUBLANES, :]

    lane = lax.broadcasted_iota(jnp.int32, (tile, LANES), 1)
    low = lane < CONV_GROUP_DIM
    for blk in range(CONV_WIDTH // LANES):
        sl = slice(blk * LANES, (blk + 1) * LANES)
        yb = y[:, sl]
        y2 = yb * yb
        ss_lo = jnp.sum(jnp.where(low, y2, 0.0), axis=-1, keepdims=True)
        ss_hi = jnp.sum(jnp.where(low, 0.0, y2), axis=-1, keepdims=True)
        inv = lax.rsqrt(jnp.where(low, ss_lo, ss_hi) * (1.0 / CONV_GROUP_DIM) + NORM_EPS)
        mix_ref[:, sl] = (yb * inv * cng_ref[:, sl]).astype(BF16)

    outs = [jnp.dot(lhs_ref[n], rhs_ref[n], preferred_element_type=F32) for n in range(len(units))]

    acc = jnp.dot(mix_ref[:, 0:CONV_WIDTH], wout_ref[0:CONV_WIDTH, :], preferred_element_type=F32)

    for hd in range(RET_HEADS):
        hs = slice(hd * HEAD_DIM, (hd + 1) * HEAD_DIM)
        o = jnp.concatenate([outs[c * RET_HEADS + hd] for c in range(n_chunks)], axis=0)
        gate = zg[:, hs]
        gate = gate * (1.0 / (1.0 + jnp.exp(-gate)))
        yr = o * _rms_scale(o) * rng_ref[:, hs] * gate
        mix_ref[:, CONV_WIDTH + hd * HEAD_DIM:CONV_WIDTH + (hd + 1) * HEAD_DIM] = yr.astype(BF16)

    h_ref[...] = x + acc + jnp.dot(mix_ref[:, CONV_WIDTH:], wout_ref[CONV_WIDTH:, :], preferred_element_type=F32)


def _mlp_kernel(h_ref, g2_ref, wup_ref, wdn_ref, gf_ref, o_ref, hid_ref):
    for r in range(h_ref.shape[0] // MLP_SUB):
        rows = slice(r * MLP_SUB, (r + 1) * MLP_SUB)
        h = h_ref[rows, :]
        u = (h * _rms_scale(h) * g2_ref[...]).astype(BF16)
        for c in range(D_FF // FF_CHUNK):
            cols = slice(c * FF_CHUNK, (c + 1) * FF_CHUNK)
            a = jnp.maximum(_dot(u, wup_ref[:, cols]), 0.0)
            hid_ref[rows, cols] = (a * a).astype(BF16)
        y = h + _dot(hid_ref[rows, :], wdn_ref[...])
        o_ref[rows, :] = y * _rms_scale(y) * gf_ref[...]


def _resident(shape):
    nd = len(shape)
    return pl.BlockSpec(shape, lambda *_: (0,) * nd, pipeline_mode=pl.Buffered(1))


@functools.lru_cache(maxsize=None)
def _retention_tables(seq):
    half = HEAD_DIM // 2
    inv_freq = 1.0 / (ROPE_BASE ** (np.arange(half, dtype=np.float64) / half))
    ang = np.arange(seq, dtype=np.float64)[:, None] * inv_freq[None, :]
    cos = np.cos(ang)
    sin = np.sin(ang)
    cos_t = np.concatenate([cos, cos], axis=-1)
    sin_t = np.concatenate([-sin, sin], axis=-1)

    log_gamma = np.log(1.0 - 2.0 ** (-5.0 - np.arange(RET_HEADS, dtype=np.float64)))
    idx = np.arange(CHUNK, dtype=np.float64)
    diff = idx[:, None] - idx[None, :]
    intra = np.where(diff[None] >= 0, np.exp(log_gamma[:, None, None] * np.maximum(diff, 0.0)[None]), 0.0)
    zeta = np.exp(log_gamma[:, None] * (CHUNK - 1 - idx)[None])
    xi = np.exp(log_gamma[:, None] * (idx + 1.0)[None])
    chunk_decay = np.exp(log_gamma * CHUNK)
    k_scale = HEAD_DIM ** -0.5
    decay_t = intra * k_scale
    zeta_t = np.broadcast_to((zeta * k_scale)[:, None, :], (RET_HEADS, HEAD_DIM, CHUNK))
    xi_t = np.broadcast_to(xi[:, :, None], (RET_HEADS, CHUNK, HEAD_DIM))
    return tuple(np.ascontiguousarray(t, dtype=np.float32)
                 for t in (cos_t, sin_t, decay_t, zeta_t, xi_t, chunk_decay))


def kernel(x, norm1_g, w_in, conv_w, conv_norm_g, ret_norm_g, w_out, norm2_g, w_up, w_down, final_norm_g):
    batch, seq, d_model = x.shape
    assert d_model == D_MODEL and w_in.shape == (D_MODEL, IN_COLS)
    assert seq % MIXER_TILE == 0 and MIXER_TILE % CHUNK == 0 and (batch * seq) % MLP_TILE == 0

    cos_t, sin_t, decay_t, zeta_t, xi_t, chunk_decay = _retention_tables(seq)
    row = lambda g: g.reshape(1, -1).astype(F32)
    n_units = (MIXER_TILE // CHUNK) * RET_HEADS

    tile_spec = pl.BlockSpec((None, MIXER_TILE, D_MODEL), lambda b, j: (b, j, 0))
    rope_spec = pl.BlockSpec((MIXER_TILE, HEAD_DIM), lambda b, j: (j, 0))
    seq_tiles = seq // MIXER_TILE
    n_steps = batch * seq_tiles
    assert D_MODEL % n_steps == 0 and D_FF % n_steps == 0
    wup_slab = pl.BlockSpec((D_MODEL // n_steps, D_FF), lambda b, j: (b * seq_tiles + j, 0))
    wdn_slab = pl.BlockSpec((D_FF // n_steps, D_MODEL), lambda b, j: (b * seq_tiles + j, 0))
    h, w_up_bf, w_down_bf = pl.pallas_call(
        _mixer_kernel,
        grid=(batch, seq_tiles),
        in_specs=[
            pl.BlockSpec(memory_space=pltpu.SMEM),
            tile_spec,
            _resident((1, D_MODEL)),
            _resident((D_MODEL, IN_COLS)),
            _resident((CONV_K, CONV_WIDTH)),
            _resident((1, CONV_WIDTH)),
            _resident((1, RET_WIDTH)),
            _resident((D_MODEL, D_MODEL)),
            rope_spec,
            rope_spec,
            _resident((RET_HEADS, CHUNK, CHUNK)),
            _resident((RET_HEADS, CHUNK, HEAD_DIM)),
            _resident((RET_HEADS, CHUNK, HEAD_DIM)),
            wup_slab,
            wdn_slab,
        ],
        out_specs=[tile_spec, wup_slab, wdn_slab],
        out_shape=[jax.ShapeDtypeStruct(x.shape, F32),
                   jax.ShapeDtypeStruct(w_up.shape, BF16),
                   jax.ShapeDtypeStruct(w_down.shape, BF16)],
        scratch_shapes=[
            pltpu.VMEM((RET_HEADS, HEAD_DIM, HEAD_DIM), F32),
            pltpu.VMEM((MIXER_TILE + SUBLANES, CONV_WIDTH), F32),
            pltpu.VMEM((MIXER_TILE + SUBLANES, CONV_WIDTH), F32),
            pltpu.VMEM((MIXER_TILE, D_MODEL), BF16),
            pltpu.VMEM((n_units, CHUNK, 2 * CHUNK), BF16),
            pltpu.VMEM((n_units, 2 * CHUNK, HEAD_DIM), BF16),
        ],
        compiler_params=pltpu.CompilerParams(
            dimension_semantics=("arbitrary", "arbitrary"),
            vmem_limit_bytes=VMEM_LIMIT_BYTES),
        name="mixer",
    )(chunk_decay, x, row(norm1_g), w_in.astype(BF16), conv_w, row(conv_norm_g), row(ret_norm_g),
      w_out.astype(BF16), cos_t, sin_t, decay_t, zeta_t, xi_t, w_up, w_down)

    tokens = batch * seq
    tok_spec = pl.BlockSpec((MLP_TILE, D_MODEL), lambda i: (i, 0))
    out = pl.pallas_call(
        _mlp_kernel,
        grid=(tokens // MLP_TILE,),
        in_specs=[
            tok_spec,
            _resident((1, D_MODEL)),
            _resident((D_MODEL, D_FF)),
            _resident((D_FF, D_MODEL)),
            _resident((1, D_MODEL)),
        ],
        out_specs=tok_spec,
        out_shape=jax.ShapeDtypeStruct((tokens, D_MODEL), F32),
        scratch_shapes=[pltpu.VMEM((MLP_TILE, D_FF), BF16)],
        compiler_params=pltpu.CompilerParams(
            dimension_semantics=("arbitrary",),
            vmem_limit_bytes=VMEM_LIMIT_BYTES),
        name="mlp",
    )(h.reshape(tokens, D_MODEL), row(norm2_g), w_up_bf, w_down_bf, row(final_norm_g))
    return out.reshape(batch, seq, D_MODEL)
```

```python
import functools

import jax
import jax.numpy as jnp
import numpy as np
from jax import lax
from jax.experimental import pallas as pl
from jax.experimental.pallas import tpu as pltpu

D_MODEL = 1024
CONV_WIDTH = 512
CONV_GROUPS = 8
CONV_GROUP_DIM = CONV_WIDTH // CONV_GROUPS
CONV_K = 3
RET_WIDTH = 512
RET_HEADS = 4
HEAD_DIM = RET_WIDTH // RET_HEADS
CHUNK = 128
ROPE_BASE = 10000.0
D_FF = 4 * D_MODEL
NORM_EPS = 1e-6
IN_COLS = 3 * CONV_WIDTH + 4 * RET_WIDTH

LANES = 128
SUBLANES = 8
VMEM_LIMIT_BYTES = 56 * 1024 * 1024

MIXER_TILE = 1024
MLP_TILE = 1024
CAST_ROWS = 64
PROJ_ROWS = 1024
MLP_SUB = 256
FF_CHUNK = 2048

BF16 = jnp.bfloat16
F32 = jnp.float32


def _dot(a, b):
    return jnp.dot(a, b, preferred_element_type=F32)


def _rms_scale(x):
    return lax.rsqrt(jnp.mean(x * x, axis=-1, keepdims=True) + NORM_EPS)


def _cast_rows_to_bf16(src_hbm, dst_ref, stage_ref, sem_ref):
    rows = stage_ref.shape[1]
    n = src_hbm.shape[0] // rows

    def copy(k):
        return pltpu.make_async_copy(src_hbm.at[pl.ds(k * rows, rows), :], stage_ref.at[k % 2], sem_ref.at[k % 2])

    copy(0).start()
    for k in range(n):
        if k + 1 < n:
            copy(k + 1).start()
        copy(k).wait()
        dst_ref[k * rows:(k + 1) * rows, :] = stage_ref[k % 2].astype(BF16)


def _mixer_kernel(cd_ref, x_ref, g1_ref, win_hbm, convw_ref, cng_ref, rng_ref, wout_hbm,
                  cos_ref, sin_ref, decay_ref, zeta_ref, xi_ref, wup_ref, wdn_ref,
                  h_ref, wup_bf_ref, wdn_bf_ref,
                  win_ref, wout_ref, win_stage_ref, wout_stage_ref, cast_sem_ref,
                  state_ref, pd1_ref, pd2_ref, mix_ref, lhs_ref, rhs_ref):
    tile = x_ref.shape[0]

    @pl.when((pl.program_id(0) == 0) & (pl.program_id(1) == 0))
    def _():
        _cast_rows_to_bf16(win_hbm, win_ref, win_stage_ref, cast_sem_ref)
        _cast_rows_to_bf16(wout_hbm, wout_ref, wout_stage_ref, cast_sem_ref)

    wup_bf_ref[...] = wup_ref[...].astype(BF16)
    wdn_bf_ref[...] = wdn_ref[...].astype(BF16)
    n_chunks = tile // CHUNK
    units = [(c, hd) for c in range(n_chunks) for hd in range(RET_HEADS)]

    @pl.when(pl.program_id(1) == 0)
    def _():
        state_ref[...] = jnp.zeros_like(state_ref)
        zeros = jnp.zeros((SUBLANES, CONV_WIDTH), F32)
        pd1_ref[0:SUBLANES, :] = zeros
        pd2_ref[0:SUBLANES, :] = zeros
        pd1_ref[tile:tile + SUBLANES, :] = zeros
        pd2_ref[tile:tile + SUBLANES, :] = zeros

    row_blocks = [slice(r * PROJ_ROWS, (r + 1) * PROJ_ROWS) for r in range(tile // PROJ_ROWS)]
    us = []
    for rb in row_blocks:
        xb = x_ref[rb, :]
        us.append((xb * _rms_scale(xb) * g1_ref[...]).astype(BF16))

    def in_proj(col0, width):
        w = win_ref[:, col0:col0 + width]
        return jnp.concatenate([jnp.dot(ub, w, preferred_element_type=F32) for ub in us], axis=0)

    c0 = 3 * CONV_WIDTH

    zq = in_proj(c0, RET_WIDTH)
    zk = in_proj(c0 + RET_WIDTH, RET_WIDTH)
    cb = in_proj(0, CONV_WIDTH)
    cc = in_proj(CONV_WIDTH, CONV_WIDTH)

    cos = cos_ref[...]
    sin = sin_ref[...]
    q_rot, kt_rot = [], []
    for hd in range(RET_HEADS):
        hs = slice(hd * HEAD_DIM, (hd + 1) * HEAD_DIM)
        qh = zq[:, hs]
        kh = zk[:, hs]
        q_rot.append(qh * cos + pltpu.roll(qh, HEAD_DIM // 2, axis=1) * sin)
        kt_rot.append((kh * cos + pltpu.roll(kh, HEAD_DIM // 2, axis=1) * sin).T)

    for n, (c, hd) in enumerate(units):
        rows = slice(c * CHUNK, (c + 1) * CHUNK)
        q = q_rot[hd][rows]
        kt = kt_rot[hd][:, rows]
        s = jnp.dot(q.astype(BF16), kt.astype(BF16), preferred_element_type=F32) * decay_ref[hd]
        lhs_ref[n, :, 0:CHUNK] = s.astype(BF16)
        lhs_ref[n, :, CHUNK:2 * CHUNK] = (q * xi_ref[hd]).astype(BF16)

    ch = in_proj(2 * CONV_WIDTH, CONV_WIDTH)
    zv = in_proj(c0 + 2 * RET_WIDTH, RET_WIDTH)
    v_bf = zv.astype(BF16)

    states = [state_ref[hd] for hd in range(RET_HEADS)]
    for n, (c, hd) in enumerate(units):
        rows = slice(c * CHUNK, (c + 1) * CHUNK)
        v = v_bf[rows, hd * HEAD_DIM:(hd + 1) * HEAD_DIM]
        kzt = (kt_rot[hd][:, rows] * zeta_ref[hd]).astype(BF16)
        kv = jnp.dot(kzt, v, preferred_element_type=F32)
        rhs_ref[n, 0:CHUNK, :] = v
        rhs_ref[n, CHUNK:2 * CHUNK, :] = states[hd].astype(BF16)
        states[hd] = cd_ref[hd] * states[hd] + kv
    for hd in range(RET_HEADS):
        state_ref[hd] = states[hd]

    zg = in_proj(c0 + 3 * RET_WIDTH, RET_WIDTH)

    p = cc * ch
    pd1_ref[1:1 + tile, :] = p
    pd2_ref[2:2 + tile, :] = p
    p1 = pd1_ref[0:tile, :]
    p2 = pd2_ref[0:tile, :]
    y = cb * (p2 * convw_ref[0:1, :] + p1 * convw_ref[1:2, :] + p * convw_ref[2:3, :])
    pd1_ref[0:SUBLANES, :] = pd1_ref[tile:tile + SUBLANES, :]
    pd2_ref[0:SUBLANES, :] = pd2_ref[tile:tile + SUBLANES, :]

    lane = lax.broadcasted_iota(jnp.int32, (tile, LANES), 1)
    low = lane < CONV_GROUP_DIM
    for blk in range(CONV_WIDTH // LANES):
        sl = slice(blk * LANES, (blk + 1) * LANES)
        yb = y[:, sl]
        y2 = yb * yb
        ss_lo = jnp.sum(jnp.where(low, y2, 0.0), axis=-1, keepdims=True)
        ss_hi = jnp.sum(jnp.where(low, 0.0, y2), axis=-1, keepdims=True)
        inv = lax.rsqrt(jnp.where(low, ss_lo, ss_hi) * (1.0 / CONV_GROUP_DIM) + NORM_EPS)
        mix_ref[:, sl] = (yb * inv * cng_ref[:, sl]).astype(BF16)

    outs = [jnp.dot(lhs_ref[n], rhs_ref[n], preferred_element_type=F32) for n in range(len(units))]

    accs = [jnp.dot(mix_ref[rb, 0:CONV_WIDTH], wout_ref[0:CONV_WIDTH, :], preferred_element_type=F32)
            for rb in row_blocks]

    for hd in range(RET_HEADS):
        hs = slice(hd * HEAD_DIM, (hd + 1) * HEAD_DIM)
        o = jnp.concatenate([outs[c * RET_HEADS + hd] for c in range(n_chunks)], axis=0)
        gate = zg[:, hs]
        gate = gate * (1.0 / (1.0 + jnp.exp(-gate)))
        yr = o * _rms_scale(o) * rng_ref[:, hs] * gate
        mix_ref[:, CONV_WIDTH + hd * HEAD_DIM:CONV_WIDTH + (hd + 1) * HEAD_DIM] = yr.astype(BF16)

    for rb, acc in zip(row_blocks, accs):
        h_ref[rb, :] = x_ref[rb, :] + acc + jnp.dot(mix_ref[rb, CONV_WIDTH:], wout_ref[CONV_WIDTH:, :],
                                                    preferred_element_type=F32)


def _mlp_kernel(h_ref, g2_ref, wup_ref, wdn_ref, gf_ref, o_ref, hid_ref):
    for r in range(h_ref.shape[0] // MLP_SUB):
        rows = slice(r * MLP_SUB, (r + 1) * MLP_SUB)
        h = h_ref[rows, :]
        u = (h * _rms_scale(h) * g2_ref[...]).astype(BF16)
        for c in range(D_FF // FF_CHUNK):
            cols = slice(c * FF_CHUNK, (c + 1) * FF_CHUNK)
            a = jnp.maximum(_dot(u, wup_ref[:, cols]), 0.0)
            hid_ref[rows, cols] = (a * a).astype(BF16)
        y = h + _dot(hid_ref[rows, :], wdn_ref[...])
        o_ref[rows, :] = y * _rms_scale(y) * gf_ref[...]


def _resident(shape):
    nd = len(shape)
    return pl.BlockSpec(shape, lambda *_: (0,) * nd, pipeline_mode=pl.Buffered(1))


@functools.lru_cache(maxsize=None)
def _retention_tables(seq):
    half = HEAD_DIM // 2
    inv_freq = 1.0 / (ROPE_BASE ** (np.arange(half, dtype=np.float64) / half))
    ang = np.arange(seq, dtype=np.float64)[:, None] * inv_freq[None, :]
    cos = np.cos(ang)
    sin = np.sin(ang)
    cos_t = np.concatenate([cos, cos], axis=-1)
    sin_t = np.concatenate([-sin, sin], axis=-1)

    log_gamma = np.log(1.0 - 2.0 ** (-5.0 - np.arange(RET_HEADS, dtype=np.float64)))
    idx = np.arange(CHUNK, dtype=np.float64)
    diff = idx[:, None] - idx[None, :]
    intra = np.where(diff[None] >= 0, np.exp(log_gamma[:, None, None] * np.maximum(diff, 0.0)[None]), 0.0)
    zeta = np.exp(log_gamma[:, None] * (CHUNK - 1 - idx)[None])
    xi = np.exp(log_gamma[:, None] * (idx + 1.0)[None])
    chunk_decay = np.exp(log_gamma * CHUNK)
    k_scale = HEAD_DIM ** -0.5
    decay_t = intra * k_scale
    zeta_t = np.broadcast_to((zeta * k_scale)[:, None, :], (RET_HEADS, HEAD_DIM, CHUNK))
    xi_t = np.broadcast_to(xi[:, :, None], (RET_HEADS, CHUNK, HEAD_DIM))
    return tuple(np.ascontiguousarray(t, dtype=np.float32)
                 for t in (cos_t, sin_t, decay_t, zeta_t, xi_t, chunk_decay))


def kernel(x, norm1_g, w_in, conv_w, conv_norm_g, ret_norm_g, w_out, norm2_g, w_up, w_down, final_norm_g):
    batch, seq, d_model = x.shape
    assert d_model == D_MODEL and w_in.shape == (D_MODEL, IN_COLS)
    assert seq % MIXER_TILE == 0 and MIXER_TILE % CHUNK == 0 and (batch * seq) % MLP_TILE == 0

    cos_t, sin_t, decay_t, zeta_t, xi_t, chunk_decay = _retention_tables(seq)
    row = lambda g: g.reshape(1, -1).astype(F32)
    n_units = (MIXER_TILE // CHUNK) * RET_HEADS

    tile_spec = pl.BlockSpec((None, MIXER_TILE, D_MODEL), lambda b, j: (b, j, 0))
    rope_spec = pl.BlockSpec((MIXER_TILE, HEAD_DIM), lambda b, j: (j, 0))
    seq_tiles = seq // MIXER_TILE
    n_steps = batch * seq_tiles
    assert D_MODEL % n_steps == 0 and D_FF % n_steps == 0
    wup_slab = pl.BlockSpec((D_MODEL // n_steps, D_FF), lambda b, j: (b * seq_tiles + j, 0))
    wdn_slab = pl.BlockSpec((D_FF // n_steps, D_MODEL), lambda b, j: (b * seq_tiles + j, 0))
    h, w_up_bf, w_down_bf = pl.pallas_call(
        _mixer_kernel,
        grid=(batch, seq_tiles),
        in_specs=[
            pl.BlockSpec(memory_space=pltpu.SMEM),
            tile_spec,
            _resident((1, D_MODEL)),
            pl.BlockSpec(memory_space=pl.ANY),
            _resident((CONV_K, CONV_WIDTH)),
            _resident((1, CONV_WIDTH)),
            _resident((1, RET_WIDTH)),
            pl.BlockSpec(memory_space=pl.ANY),
            rope_spec,
            rope_spec,
            _resident((RET_HEADS, CHUNK, CHUNK)),
            _resident((RET_HEADS, CHUNK, HEAD_DIM)),
            _resident((RET_HEADS, CHUNK, HEAD_DIM)),
            wup_slab,
            wdn_slab,
        ],
        out_specs=[tile_spec, wup_slab, wdn_slab],
        out_shape=[jax.ShapeDtypeStruct(x.shape, F32),
                   jax.ShapeDtypeStruct(w_up.shape, BF16),
                   jax.ShapeDtypeStruct(w_down.shape, BF16)],
        scratch_shapes=[
            pltpu.VMEM((D_MODEL, IN_COLS), BF16),
            pltpu.VMEM((D_MODEL, D_MODEL), BF16),
            pltpu.VMEM((2, CAST_ROWS, IN_COLS), F32),
            pltpu.VMEM((2, CAST_ROWS, D_MODEL), F32),
            pltpu.SemaphoreType.DMA((2,)),
            pltpu.VMEM((RET_HEADS, HEAD_DIM, HEAD_DIM), F32),
            pltpu.VMEM((MIXER_TILE + SUBLANES, CONV_WIDTH), F32),
            pltpu.VMEM((MIXER_TILE + SUBLANES, CONV_WIDTH), F32),
            pltpu.VMEM((MIXER_TILE, D_MODEL), BF16),
            pltpu.VMEM((n_units, CHUNK, 2 * CHUNK), BF16),
            pltpu.VMEM((n_units, 2 * CHUNK, HEAD_DIM), BF16),
        ],
        compiler_params=pltpu.CompilerParams(
            dimension_semantics=("arbitrary", "arbitrary"),
            vmem_limit_bytes=VMEM_LIMIT_BYTES),
        name="mixer",
    )(chunk_decay, x, row(norm1_g), w_in, conv_w, row(conv_norm_g), row(ret_norm_g),
      w_out, cos_t, sin_t, decay_t, zeta_t, xi_t, w_up, w_down)

    tokens = batch * seq
    tok_spec = pl.BlockSpec((MLP_TILE, D_MODEL), lambda i: (i, 0))
    out = pl.pallas_call(
        _mlp_kernel,
        grid=(tokens // MLP_TILE,),
        in_specs=[
            tok_spec,
            _resident((1, D_MODEL)),
            _resident((D_MODEL, D_FF)),
            _resident((D_FF, D_MODEL)),
            _resident((1, D_MODEL)),
        ],
        out_specs=tok_spec,
        out_shape=jax.ShapeDtypeStruct((tokens, D_MODEL), F32),
        scratch_shapes=[pltpu.VMEM((MLP_TILE, D_FF), BF16)],
        compiler_params=pltpu.CompilerParams(
            dimension_semantics=("arbitrary",),
            vmem_limit_bytes=VMEM_LIMIT_BYTES),
        name="mlp",
    )(h.reshape(tokens, D_MODEL), row(norm2_g), w_up_bf, w_down_bf, row(final_norm_g))
    return out.reshape(batch, seq, D_MODEL)
```

```python
import functools

import jax
import jax.numpy as jnp
import numpy as np
from jax import lax
from jax.experimental import pallas as pl
from jax.experimental.pallas import tpu as pltpu

D_MODEL = 1024
CONV_WIDTH = 512
CONV_GROUPS = 8
CONV_GROUP_DIM = CONV_WIDTH // CONV_GROUPS
CONV_K = 3
RET_WIDTH = 512
RET_HEADS = 4
HEAD_DIM = RET_WIDTH // RET_HEADS
CHUNK = 128
ROPE_BASE = 10000.0
D_FF = 4 * D_MODEL
NORM_EPS = 1e-6
IN_COLS = 3 * CONV_WIDTH + 4 * RET_WIDTH

LANES = 128
SUBLANES = 8
VMEM_LIMIT_BYTES = 56 * 1024 * 1024

MIXER_TILE = 1024
MLP_TILE = 1024
CAST_ROWS = 32
CAST_SLOTS = 8
PROJ_ROWS = 1024
MLP_SUB = 256
FF_CHUNK = 2048

BF16 = jnp.bfloat16
F32 = jnp.float32


def _dot(a, b):
    return jnp.dot(a, b, preferred_element_type=F32)


def _rms_scale(x):
    return lax.rsqrt(jnp.mean(x * x, axis=-1, keepdims=True) + NORM_EPS)


def _cast_rows_to_bf16(src_hbm, dst_ref, stage_ref, sem_ref):
    slots, rows = stage_ref.shape[0], stage_ref.shape[1]
    n = src_hbm.shape[0] // rows

    def copy(k):
        slot = k % slots
        return pltpu.make_async_copy(src_hbm.at[pl.ds(k * rows, rows), :], stage_ref.at[slot], sem_ref.at[slot])

    for k in range(min(slots, n)):
        copy(k).start()
    for k in range(n):
        copy(k).wait()
        dst_ref[k * rows:(k + 1) * rows, :] = stage_ref[k % slots].astype(BF16)
        if k + slots < n:
            copy(k + slots).start()


def _mixer_kernel(cd_ref, x_ref, g1_ref, win_hbm, convw_ref, cng_ref, rng_ref, wout_hbm,
                  cos_ref, sin_ref, decay_ref, zeta_ref, xi_ref, wup_ref, wdn_ref,
                  h_ref, wup_bf_ref, wdn_bf_ref,
                  win_ref, wout_ref, win_stage_ref, wout_stage_ref, cast_sem_ref,
                  state_ref, pd1_ref, pd2_ref, mix_ref, lhs_ref, rhs_ref):
    tile = x_ref.shape[0]

    @pl.when((pl.program_id(0) == 0) & (pl.program_id(1) == 0))
    def _():
        _cast_rows_to_bf16(win_hbm, win_ref, win_stage_ref, cast_sem_ref)
        _cast_rows_to_bf16(wout_hbm, wout_ref, wout_stage_ref, cast_sem_ref)

    wup_bf_ref[...] = wup_ref[...].astype(BF16)
    wdn_bf_ref[...] = wdn_ref[...].astype(BF16)
    n_chunks = tile // CHUNK
    units = [(c, hd) for c in range(n_chunks) for hd in range(RET_HEADS)]

    @pl.when(pl.program_id(1) == 0)
    def _():
        state_ref[...] = jnp.zeros_like(state_ref)
        zeros = jnp.zeros((SUBLANES, CONV_WIDTH), F32)
        pd1_ref[0:SUBLANES, :] = zeros
        pd2_ref[0:SUBLANES, :] = zeros
        pd1_ref[tile:tile + SUBLANES, :] = zeros
        pd2_ref[tile:tile + SUBLANES, :] = zeros

    row_blocks = [slice(r * PROJ_ROWS, (r + 1) * PROJ_ROWS) for r in range(tile // PROJ_ROWS)]
    us = []
    for rb in row_blocks:
        xb = x_ref[rb, :]
        us.append((xb * _rms_scale(xb) * g1_ref[...]).astype(BF16))

    def in_proj(col0, width):
        w = win_ref[:, col0:col0 + width]
        return jnp.concatenate([jnp.dot(ub, w, preferred_element_type=F32) for ub in us], axis=0)

    c0 = 3 * CONV_WIDTH

    zq = in_proj(c0, RET_WIDTH)
    zk = in_proj(c0 + RET_WIDTH, RET_WIDTH)
    cb = in_proj(0, CONV_WIDTH)
    cc = in_proj(CONV_WIDTH, CONV_WIDTH)

    cos = cos_ref[...]
    sin = sin_ref[...]
    q_rot, kt_rot = [], []
    for hd in range(RET_HEADS):
        hs = slice(hd * HEAD_DIM, (hd + 1) * HEAD_DIM)
        qh = zq[:, hs]
        kh = zk[:, hs]
        q_rot.append(qh * cos + pltpu.roll(qh, HEAD_DIM // 2, axis=1) * sin)
        kt_rot.append((kh * cos + pltpu.roll(kh, HEAD_DIM // 2, axis=1) * sin).T)

    for n, (c, hd) in enumerate(units):
        rows = slice(c * CHUNK, (c + 1) * CHUNK)
        q = q_rot[hd][rows]
        kt = kt_rot[hd][:, rows]
        s = jnp.dot(q.astype(BF16), kt.astype(BF16), preferred_element_type=F32) * decay_ref[hd]
        lhs_ref[n, :, 0:CHUNK] = s.astype(BF16)
        lhs_ref[n, :, CHUNK:2 * CHUNK] = (q * xi_ref[hd]).astype(BF16)

    ch = in_proj(2 * CONV_WIDTH, CONV_WIDTH)
    zv = in_proj(c0 + 2 * RET_WIDTH, RET_WIDTH)
    v_bf = zv.astype(BF16)

    states = [state_ref[hd] for hd in range(RET_HEADS)]
    for n, (c, hd) in enumerate(units):
        rows = slice(c * CHUNK, (c + 1) * CHUNK)
        v = v_bf[rows, hd * HEAD_DIM:(hd + 1) * HEAD_DIM]
        kzt = (kt_rot[hd][:, rows] * zeta_ref[hd]).astype(BF16)
        kv = jnp.dot(kzt, v, preferred_element_type=F32)
        rhs_ref[n, 0:CHUNK, :] = v
        rhs_ref[n, CHUNK:2 * CHUNK, :] = states[hd].astype(BF16)
        states[hd] = cd_ref[hd] * states[hd] + kv
    for hd in range(RET_HEADS):
        state_ref[hd] = states[hd]

    zg = in_proj(c0 + 3 * RET_WIDTH, RET_WIDTH)

    p = cc * ch
    pd1_ref[1:1 + tile, :] = p
    pd2_ref[2:2 + tile, :] = p
    p1 = pd1_ref[0:tile, :]
    p2 = pd2_ref[0:tile, :]
    y = cb * (p2 * convw_ref[0:1, :] + p1 * convw_ref[1:2, :] + p * convw_ref[2:3, :])
    pd1_ref[0:SUBLANES, :] = pd1_ref[tile:tile + SUBLANES, :]
    pd2_ref[0:SUBLANES, :] = pd2_ref[tile:tile + SUBLANES, :]

    lane = lax.broadcasted_iota(jnp.int32, (tile, LANES), 1)
    low = lane < CONV_GROUP_DIM
    for blk in range(CONV_WIDTH // LANES):
        sl = slice(blk * LANES, (blk + 1) * LANES)
        yb = y[:, sl]
        y2 = yb * yb
        ss_lo = jnp.sum(jnp.where(low, y2, 0.0), axis=-1, keepdims=True)
        ss_hi = jnp.sum(jnp.where(low, 0.0, y2), axis=-1, keepdims=True)
        inv = lax.rsqrt(jnp.where(low, ss_lo, ss_hi) * (1.0 / CONV_GROUP_DIM) + NORM_EPS)
        mix_ref[:, sl] = (yb * inv * cng_ref[:, sl]).astype(BF16)

    outs = [jnp.dot(lhs_ref[n], rhs_ref[n], preferred_element_type=F32) for n in range(len(units))]

    accs = [jnp.dot(mix_ref[rb, 0:CONV_WIDTH], wout_ref[0:CONV_WIDTH, :], preferred_element_type=F32)
            for rb in row_blocks]

    for hd in range(RET_HEADS):
        hs = slice(hd * HEAD_DIM, (hd + 1) * HEAD_DIM)
        o = jnp.concatenate([outs[c * RET_HEADS + hd] for c in range(n_chunks)], axis=0)
        gate = zg[:, hs]
        gate = gate * (1.0 / (1.0 + jnp.exp(-gate)))
        yr = o * _rms_scale(o) * rng_ref[:, hs] * gate
        mix_ref[:, CONV_WIDTH + hd * HEAD_DIM:CONV_WIDTH + (hd + 1) * HEAD_DIM] = yr.astype(BF16)

    for rb, acc in zip(row_blocks, accs):
        h_ref[rb, :] = x_ref[rb, :] + acc + jnp.dot(mix_ref[rb, CONV_WIDTH:], wout_ref[CONV_WIDTH:, :],
                                                    preferred_element_type=F32)


def _mlp_kernel(h_ref, g2_ref, wup_ref, wdn_ref, gf_ref, o_ref, hid_ref):
    for r in range(h_ref.shape[0] // MLP_SUB):
        rows = slice(r * MLP_SUB, (r + 1) * MLP_SUB)
        h = h_ref[rows, :]
        u = (h * _rms_scale(h) * g2_ref[...]).astype(BF16)
        for c in range(D_FF // FF_CHUNK):
            cols = slice(c * FF_CHUNK, (c + 1) * FF_CHUNK)
            a = jnp.maximum(_dot(u, wup_ref[:, cols]), 0.0)
            hid_ref[rows, cols] = (a * a).astype(BF16)
        y = h + _dot(hid_ref[rows, :], wdn_ref[...])
        o_ref[rows, :] = y * _rms_scale(y) * gf_ref[...]


def _resident(shape):
    nd = len(shape)
    return pl.BlockSpec(shape, lambda *_: (0,) * nd, pipeline_mode=pl.Buffered(1))


@functools.lru_cache(maxsize=None)
def _retention_tables(seq):
    half = HEAD_DIM // 2
    inv_freq = 1.0 / (ROPE_BASE ** (np.arange(half, dtype=np.float64) / half))
    ang = np.arange(seq, dtype=np.float64)[:, None] * inv_freq[None, :]
    cos = np.cos(ang)
    sin = np.sin(ang)
    cos_t = np.concatenate([cos, cos], axis=-1)
    sin_t = np.concatenate([-sin, sin], axis=-1)

    log_gamma = np.log(1.0 - 2.0 ** (-5.0 - np.arange(RET_HEADS, dtype=np.float64)))
    idx = np.arange(CHUNK, dtype=np.float64)
    diff = idx[:, None] - idx[None, :]
    intra = np.where(diff[None] >= 0, np.exp(log_gamma[:, None, None] * np.maximum(diff, 0.0)[None]), 0.0)
    zeta = np.exp(log_gamma[:, None] * (CHUNK - 1 - idx)[None])
    xi = np.exp(log_gamma[:, None] * (idx + 1.0)[None])
    chunk_decay = np.exp(log_gamma * CHUNK)
    k_scale = HEAD_DIM ** -0.5
    decay_t = intra * k_scale
    zeta_t = np.broadcast_to((zeta * k_scale)[:, None, :], (RET_HEADS, HEAD_DIM, CHUNK))
    xi_t = np.broadcast_to(xi[:, :, None], (RET_HEADS, CHUNK, HEAD_DIM))
    return tuple(np.ascontiguousarray(t, dtype=np.float32)
                 for t in (cos_t, sin_t, decay_t, zeta_t, xi_t, chunk_decay))


def kernel(x, norm1_g, w_in, conv_w, conv_norm_g, ret_norm_g, w_out, norm2_g, w_up, w_down, final_norm_g):
    batch, seq, d_model = x.shape
    assert d_model == D_MODEL and w_in.shape == (D_MODEL, IN_COLS)
    assert seq % MIXER_TILE == 0 and MIXER_TILE % CHUNK == 0 and (batch * seq) % MLP_TILE == 0

    cos_t, sin_t, decay_t, zeta_t, xi_t, chunk_decay = _retention_tables(seq)
    row = lambda g: g.reshape(1, -1).astype(F32)
    n_units = (MIXER_TILE // CHUNK) * RET_HEADS

    tile_spec = pl.BlockSpec((None, MIXER_TILE, D_MODEL), lambda b, j: (b, j, 0))
    rope_spec = pl.BlockSpec((MIXER_TILE, HEAD_DIM), lambda b, j: (j, 0))
    seq_tiles = seq // MIXER_TILE
    n_steps = batch * seq_tiles
    assert D_MODEL % n_steps == 0 and D_FF % n_steps == 0
    wup_slab = pl.BlockSpec((D_MODEL // n_steps, D_FF), lambda b, j: (b * seq_tiles + j, 0))
    wdn_slab = pl.BlockSpec((D_FF // n_steps, D_MODEL), lambda b, j: (b * seq_tiles + j, 0))
    h, w_up_bf, w_down_bf = pl.pallas_call(
        _mixer_kernel,
        grid=(batch, seq_tiles),
        in_specs=[
            pl.BlockSpec(memory_space=pltpu.SMEM),
            tile_spec,
            _resident((1, D_MODEL)),
            pl.BlockSpec(memory_space=pl.ANY),
            _resident((CONV_K, CONV_WIDTH)),
            _resident((1, CONV_WIDTH)),
            _resident((1, RET_WIDTH)),
            pl.BlockSpec(memory_space=pl.ANY),
            rope_spec,
            rope_spec,
            _resident((RET_HEADS, CHUNK, CHUNK)),
            _resident((RET_HEADS, CHUNK, HEAD_DIM)),
            _resident((RET_HEADS, CHUNK, HEAD_DIM)),
            wup_slab,
            wdn_slab,
        ],
        out_specs=[tile_spec, wup_slab, wdn_slab],
        out_shape=[jax.ShapeDtypeStruct(x.shape, F32),
                   jax.ShapeDtypeStruct(w_up.shape, BF16),
                   jax.ShapeDtypeStruct(w_down.shape, BF16)],
        scratch_shapes=[
            pltpu.VMEM((D_MODEL, IN_COLS), BF16),
            pltpu.VMEM((D_MODEL, D_MODEL), BF16),
            pltpu.VMEM((CAST_SLOTS, CAST_ROWS, IN_COLS), F32),
            pltpu.VMEM((CAST_SLOTS, CAST_ROWS, D_MODEL), F32),
            pltpu.SemaphoreType.DMA((CAST_SLOTS,)),
            pltpu.VMEM((RET_HEADS, HEAD_DIM, HEAD_DIM), F32),
            pltpu.VMEM((MIXER_TILE + SUBLANES, CONV_WIDTH), F32),
            pltpu.VMEM((MIXER_TILE + SUBLANES, CONV_WIDTH), F32),
            pltpu.VMEM((MIXER_TILE, D_MODEL), BF16),
            pltpu.VMEM((n_units, CHUNK, 2 * CHUNK), BF16),
            pltpu.VMEM((n_units, 2 * CHUNK, HEAD_DIM), BF16),
        ],
        compiler_params=pltpu.CompilerParams(
            dimension_semantics=("arbitrary", "arbitrary"),
            vmem_limit_bytes=VMEM_LIMIT_BYTES),
        name="mixer",
    )(chunk_decay, x, row(norm1_g), w_in, conv_w, row(conv_norm_g), row(ret_norm_g),
      w_out, cos_t, sin_t, decay_t, zeta_t, xi_t, w_up, w_down)

    tokens = batch * seq
    tok_spec = pl.BlockSpec((MLP_TILE, D_MODEL), lambda i: (i, 0))
    out = pl.pallas_call(
        _mlp_kernel,
        grid=(tokens // MLP_TILE,),
        in_specs=[
            tok_spec,
            _resident((1, D_MODEL)),
            _resident((D_MODEL, D_FF)),
            _resident((D_FF, D_MODEL)),
            _resident((1, D_MODEL)),
        ],
        out_specs=tok_spec,
        out_shape=jax.ShapeDtypeStruct((tokens, D_MODEL), F32),
        scratch_shapes=[pltpu.VMEM((MLP_TILE, D_FF), BF16)],
        compiler_params=pltpu.CompilerParams(
            dimension_semantics=("arbitrary",),
            vmem_limit_bytes=VMEM_LIMIT_BYTES),
        name="mlp",
    )(h.reshape(tokens, D_MODEL), row(norm2_g), w_up_bf, w_down_bf, row(final_norm_g))
    return out.reshape(batch, seq, D_MODEL)
```

```python
import functools

import jax
import jax.numpy as jnp
import numpy as np
from jax import lax
from jax.experimental import pallas as pl
from jax.experimental.pallas import tpu as pltpu

D_MODEL = 1024
CONV_WIDTH = 512
CONV_GROUPS = 8
CONV_GROUP_DIM = CONV_WIDTH // CONV_GROUPS
CONV_K = 3
RET_WIDTH = 512
RET_HEADS = 4
HEAD_DIM = RET_WIDTH // RET_HEADS
CHUNK = 128
ROPE_BASE = 10000.0
D_FF = 4 * D_MODEL
NORM_EPS = 1e-6
IN_COLS = 3 * CONV_WIDTH + 4 * RET_WIDTH

LANES = 128
SUBLANES = 8
VMEM_LIMIT_BYTES = 56 * 1024 * 1024

MIXER_TILE = 1024
MLP_TILE = 1024
MLP_SUB = 512
FF_CHUNK = 1024
CAST_ROWS = 32
CAST_SLOTS = 8

BF16 = jnp.bfloat16
F32 = jnp.float32


def _dot(a, b):
    return jnp.dot(a, b, preferred_element_type=F32)


def _rms_scale(x):
    return lax.rsqrt(jnp.mean(x * x, axis=-1, keepdims=True) + NORM_EPS)


def _cast_rows_to_bf16(src_hbm, dst_ref, stage_ref, sem_ref):
    slots, rows = stage_ref.shape[0], stage_ref.shape[1]
    n = src_hbm.shape[0] // rows

    def copy(k):
        slot = k % slots
        return pltpu.make_async_copy(src_hbm.at[pl.ds(k * rows, rows), :], stage_ref.at[slot], sem_ref.at[slot])

    for k in range(min(slots, n)):
        copy(k).start()
    for k in range(n):
        copy(k).wait()
        dst_ref[k * rows:(k + 1) * rows, :] = stage_ref[k % slots].astype(BF16)
        if k + slots < n:
            copy(k + slots).start()


def _mixer_kernel(cd_ref, x_ref, g1_ref, win_hbm, convw_ref, cng_ref, rng_ref, wout_hbm,
                  cos_ref, sin_ref, decay_ref, zeta_ref, xi_ref, wup_ref, wdn_ref,
                  h_ref, wup_bf_ref, wdn_bf_ref,
                  win_ref, wout_ref, win_stage_ref, wout_stage_ref, cast_sem_ref,
                  state_ref, pd1_ref, pd2_ref, mix_ref, lhs_ref, rhs_ref):
    tile = x_ref.shape[0]
    n_chunks = tile // CHUNK
    units = [(c, hd) for c in range(n_chunks) for hd in range(RET_HEADS)]

    @pl.when((pl.program_id(0) == 0) & (pl.program_id(1) == 0))
    def _():
        _cast_rows_to_bf16(win_hbm, win_ref, win_stage_ref, cast_sem_ref)
        _cast_rows_to_bf16(wout_hbm, wout_ref, wout_stage_ref, cast_sem_ref)

    wup_bf_ref[...] = wup_ref[...].astype(BF16)
    wdn_bf_ref[...] = wdn_ref[...].astype(BF16)

    @pl.when(pl.program_id(1) == 0)
    def _():
        state_ref[...] = jnp.zeros_like(state_ref)
        zeros = jnp.zeros((SUBLANES, CONV_WIDTH), F32)
        pd1_ref[0:SUBLANES, :] = zeros
        pd2_ref[0:SUBLANES, :] = zeros
        pd1_ref[tile:tile + SUBLANES, :] = zeros
        pd2_ref[tile:tile + SUBLANES, :] = zeros

    x = x_ref[...]
    u = (x * _rms_scale(x) * g1_ref[...]).astype(BF16)

    def in_proj(col0, width):
        return jnp.dot(u, win_ref[:, col0:col0 + width], preferred_element_type=F32)

    c0 = 3 * CONV_WIDTH

    zq = in_proj(c0, RET_WIDTH)
    zk = in_proj(c0 + RET_WIDTH, RET_WIDTH)
    cb = in_proj(0, CONV_WIDTH)
    cc = in_proj(CONV_WIDTH, CONV_WIDTH)

    cos = cos_ref[...]
    sin = sin_ref[...]
    q_rot, kt_rot = [], []
    for hd in range(RET_HEADS):
        hs = slice(hd * HEAD_DIM, (hd + 1) * HEAD_DIM)
        qh = zq[:, hs]
        kh = zk[:, hs]
        q_rot.append(qh * cos + pltpu.roll(qh, HEAD_DIM // 2, axis=1) * sin)
        kt_rot.append((kh * cos + pltpu.roll(kh, HEAD_DIM // 2, axis=1) * sin).T)

    lhs, rhs = [], []
    for n, (c, hd) in enumerate(units):
        rows = slice(c * CHUNK, (c + 1) * CHUNK)
        q = q_rot[hd][rows]
        kt = kt_rot[hd][:, rows]
        s = jnp.dot(q.astype(BF16), kt.astype(BF16), preferred_element_type=F32) * decay_ref[hd]
        lhs.append(jnp.concatenate([s.astype(BF16), (q * xi_ref[hd]).astype(BF16)], axis=1))

    ch = in_proj(2 * CONV_WIDTH, CONV_WIDTH)
    zv = in_proj(c0 + 2 * RET_WIDTH, RET_WIDTH)
    v_bf = zv.astype(BF16)

    states = [state_ref[hd] for hd in range(RET_HEADS)]
    for n, (c, hd) in enumerate(units):
        rows = slice(c * CHUNK, (c + 1) * CHUNK)
        v = v_bf[rows, hd * HEAD_DIM:(hd + 1) * HEAD_DIM]
        kzt = (kt_rot[hd][:, rows] * zeta_ref[hd]).astype(BF16)
        kv = jnp.dot(kzt, v, preferred_element_type=F32)
        rhs.append(jnp.concatenate([v, states[hd].astype(BF16)], axis=0))
        states[hd] = cd_ref[hd] * states[hd] + kv
    for hd in range(RET_HEADS):
        state_ref[hd] = states[hd]

    zg = in_proj(c0 + 3 * RET_WIDTH, RET_WIDTH)

    p = cc * ch
    pd1_ref[1:1 + tile, :] = p
    pd2_ref[2:2 + tile, :] = p
    p1 = pd1_ref[0:tile, :]
    p2 = pd2_ref[0:tile, :]
    y = cb * (p2 * convw_ref[0:1, :] + p1 * convw_ref[1:2, :] + p * convw_ref[2:3, :])
    pd1_ref[0:SUBLANES, :] = pd1_ref[tile:tile + SUBLANES, :]
    pd2_ref[0:SUBLANES, :] = pd2_ref[tile:tile + SUBLANES, :]

    lane = lax.broadcasted_iota(jnp.int32, (tile, LANES), 1)
    low = lane < CONV_GROUP_DIM
    for blk in range(CONV_WIDTH // LANES):
        sl = slice(blk * LANES, (blk + 1) * LANES)
        yb = y[:, sl]
        y2 = yb * yb
        ss_lo = jnp.sum(jnp.where(low, y2, 0.0), axis=-1, keepdims=True)
        ss_hi = jnp.sum(jnp.where(low, 0.0, y2), axis=-1, keepdims=True)
        inv = lax.rsqrt(jnp.where(low, ss_lo, ss_hi) * (1.0 / CONV_GROUP_DIM) + NORM_EPS)
        mix_ref[:, sl] = (yb * inv * cng_ref[:, sl]).astype(BF16)

    outs = [jnp.dot(lhs[n], rhs[n], preferred_element_type=F32) for n in range(len(units))]

    acc = jnp.dot(mix_ref[:, 0:CONV_WIDTH], wout_ref[0:CONV_WIDTH, :], preferred_element_type=F32)

    for hd in range(RET_HEADS):
        hs = slice(hd * HEAD_DIM, (hd + 1) * HEAD_DIM)
        o = jnp.concatenate([outs[c * RET_HEADS + hd] for c in range(n_chunks)], axis=0)
        gate = zg[:, hs]
        gate = gate * (1.0 / (1.0 + jnp.exp(-gate)))
        yr = o * _rms_scale(o) * rng_ref[:, hs] * gate
        mix_ref[:, CONV_WIDTH + hd * HEAD_DIM:CONV_WIDTH + (hd + 1) * HEAD_DIM] = yr.astype(BF16)

    h_ref[...] = x + acc + jnp.dot(mix_ref[:, CONV_WIDTH:], wout_ref[CONV_WIDTH:, :], preferred_element_type=F32)


def _mlp_kernel(h_ref, g2_ref, wup_ref, wdn_ref, gf_ref, o_ref, hid_ref):
    for r in range(h_ref.shape[0] // MLP_SUB):
        rows = slice(r * MLP_SUB, (r + 1) * MLP_SUB)
        h = h_ref[rows, :]
        u = (h * _rms_scale(h) * g2_ref[...]).astype(BF16)
        for c in range(D_FF // FF_CHUNK):
            cols = slice(c * FF_CHUNK, (c + 1) * FF_CHUNK)
            a = jnp.maximum(_dot(u, wup_ref[:, cols]), 0.0)
            hid_ref[rows, cols] = (a * a).astype(BF16)
        y = h + _dot(hid_ref[rows, :], wdn_ref[...])
        o_ref[rows, :] = y * _rms_scale(y) * gf_ref[...]


def _resident(shape):
    nd = len(shape)
    return pl.BlockSpec(shape, lambda *_: (0,) * nd, pipeline_mode=pl.Buffered(1))


@functools.lru_cache(maxsize=None)
def _retention_tables(seq):
    half = HEAD_DIM // 2
    inv_freq = 1.0 / (ROPE_BASE ** (np.arange(half, dtype=np.float64) / half))
    ang = np.arange(seq, dtype=np.float64)[:, None] * inv_freq[None, :]
    cos = np.cos(ang)
    sin = np.sin(ang)
    cos_t = np.concatenate([cos, cos], axis=-1)
    sin_t = np.concatenate([-sin, sin], axis=-1)

    log_gamma = np.log(1.0 - 2.0 ** (-5.0 - np.arange(RET_HEADS, dtype=np.float64)))
    idx = np.arange(CHUNK, dtype=np.float64)
    diff = idx[:, None] - idx[None, :]
    intra = np.where(diff[None] >= 0, np.exp(log_gamma[:, None, None] * np.maximum(diff, 0.0)[None]), 0.0)
    zeta = np.exp(log_gamma[:, None] * (CHUNK - 1 - idx)[None])
    xi = np.exp(log_gamma[:, None] * (idx + 1.0)[None])
    chunk_decay = np.exp(log_gamma * CHUNK)
    k_scale = HEAD_DIM ** -0.5
    decay_t = intra * k_scale
    zeta_t = np.broadcast_to((zeta * k_scale)[:, None, :], (RET_HEADS, HEAD_DIM, CHUNK))
    xi_t = np.broadcast_to(xi[:, :, None], (RET_HEADS, CHUNK, HEAD_DIM))
    return tuple(np.ascontiguousarray(t, dtype=np.float32)
                 for t in (cos_t, sin_t, decay_t, zeta_t, xi_t, chunk_decay))


def kernel(x, norm1_g, w_in, conv_w, conv_norm_g, ret_norm_g, w_out, norm2_g, w_up, w_down, final_norm_g):
    batch, seq, d_model = x.shape
    assert d_model == D_MODEL and w_in.shape == (D_MODEL, IN_COLS)
    assert seq % MIXER_TILE == 0 and MIXER_TILE % CHUNK == 0 and (batch * seq) % MLP_TILE == 0

    cos_t, sin_t, decay_t, zeta_t, xi_t, chunk_decay = _retention_tables(seq)
    row = lambda g: g.reshape(1, -1).astype(F32)
    n_units = (MIXER_TILE // CHUNK) * RET_HEADS

    tile_spec = pl.BlockSpec((None, MIXER_TILE, D_MODEL), lambda b, j: (b, j, 0))
    rope_spec = pl.BlockSpec((MIXER_TILE, HEAD_DIM), lambda b, j: (j, 0))
    seq_tiles = seq // MIXER_TILE
    n_steps = batch * seq_tiles
    assert D_MODEL % n_steps == 0 and D_FF % n_steps == 0
    wup_slab = pl.BlockSpec((D_MODEL // n_steps, D_FF), lambda b, j: (b * seq_tiles + j, 0))
    wdn_slab = pl.BlockSpec((D_FF // n_steps, D_MODEL), lambda b, j: (b * seq_tiles + j, 0))
    h, w_up_bf, w_down_bf = pl.pallas_call(
        _mixer_kernel,
        grid=(batch, seq_tiles),
        in_specs=[
            pl.BlockSpec(memory_space=pltpu.SMEM),
            tile_spec,
            _resident((1, D_MODEL)),
            pl.BlockSpec(memory_space=pl.ANY),
            _resident((CONV_K, CONV_WIDTH)),
            _resident((1, CONV_WIDTH)),
            _resident((1, RET_WIDTH)),
            pl.BlockSpec(memory_space=pl.ANY),
            rope_spec,
            rope_spec,
            _resident((RET_HEADS, CHUNK, CHUNK)),
            _resident((RET_HEADS, CHUNK, HEAD_DIM)),
            _resident((RET_HEADS, CHUNK, HEAD_DIM)),
            wup_slab,
            wdn_slab,
        ],
        out_specs=[tile_spec, wup_slab, wdn_slab],
        out_shape=[jax.ShapeDtypeStruct(x.shape, F32),
                   jax.ShapeDtypeStruct(w_up.shape, BF16),
                   jax.ShapeDtypeStruct(w_down.shape, BF16)],
        scratch_shapes=[
            pltpu.VMEM((D_MODEL, IN_COLS), BF16),
            pltpu.VMEM((D_MODEL, D_MODEL), BF16),
            pltpu.VMEM((CAST_SLOTS, CAST_ROWS, IN_COLS), F32),
            pltpu.VMEM((CAST_SLOTS, CAST_ROWS, D_MODEL), F32),
            pltpu.SemaphoreType.DMA((CAST_SLOTS,)),
            pltpu.VMEM((RET_HEADS, HEAD_DIM, HEAD_DIM), F32),
            pltpu.VMEM((MIXER_TILE + SUBLANES, CONV_WIDTH), F32),
            pltpu.VMEM((MIXER_TILE + SUBLANES, CONV_WIDTH), F32),
            pltpu.VMEM((MIXER_TILE, D_MODEL), BF16),
            pltpu.VMEM((n_units, CHUNK, 2 * CHUNK), BF16),
            pltpu.VMEM((n_units, 2 * CHUNK, HEAD_DIM), BF16),
        ],
        compiler_params=pltpu.CompilerParams(
            dimension_semantics=("arbitrary", "arbitrary"),
            vmem_limit_bytes=VMEM_LIMIT_BYTES),
        name="mixer",
    )(chunk_decay, x, row(norm1_g), w_in, conv_w, row(conv_norm_g), row(ret_norm_g),
      w_out, cos_t, sin_t, decay_t, zeta_t, xi_t, w_up, w_down)

    tokens = batch * seq
    tok_spec = pl.BlockSpec((MLP_TILE, D_MODEL), lambda i: (i, 0))
    out = pl.pallas_call(
        _mlp_kernel,
        grid=(tokens // MLP_TILE,),
        in_specs=[
            tok_spec,
            _resident((1, D_MODEL)),
            _resident((D_MODEL, D_FF)),
            _resident((D_FF, D_MODEL)),
            _resident((1, D_MODEL)),
        ],
        out_specs=tok_spec,
        out_shape=jax.ShapeDtypeStruct((tokens, D_MODEL), F32),
        scratch_shapes=[pltpu.VMEM((MLP_TILE, D_FF), BF16)],
        compiler_params=pltpu.CompilerParams(
            dimension_semantics=("arbitrary",),
            vmem_limit_bytes=VMEM_LIMIT_BYTES),
        name="mlp",
    )(h.reshape(tokens, D_MODEL), row(norm2_g), w_up_bf, w_down_bf, row(final_norm_g))
    return out.reshape(batch, seq, D_MODEL)
```

```python
import functools

import jax
import jax.numpy as jnp
import numpy as np
from jax import lax
from jax.experimental import pallas as pl
from jax.experimental.pallas import tpu as pltpu

D_MODEL = 1024
CONV_WIDTH = 512
CONV_GROUPS = 8
CONV_GROUP_DIM = CONV_WIDTH // CONV_GROUPS
CONV_K = 3
RET_WIDTH = 512
RET_HEADS = 4
HEAD_DIM = RET_WIDTH // RET_HEADS
CHUNK = 128
ROPE_BASE = 10000.0
D_FF = 4 * D_MODEL
NORM_EPS = 1e-6
IN_COLS = 3 * CONV_WIDTH + 4 * RET_WIDTH

LANES = 128
SUBLANES = 8
VMEM_LIMIT_BYTES = 56 * 1024 * 1024

MIXER_TILE = 1024
MLP_TILE = 1024
PROJ_ROWS = 1024
MLP_SUB = 512
FF_CHUNK = 1024
CAST_ROWS = 32
CAST_SLOTS = 8

BF16 = jnp.bfloat16
F32 = jnp.float32


def _dot(a, b):
    return jnp.dot(a, b, preferred_element_type=F32)


def _rms_scale(x):
    return lax.rsqrt(jnp.mean(x * x, axis=-1, keepdims=True) + NORM_EPS)


def _cast_rows_to_bf16(src_hbm, dst_ref, stage_ref, sem_ref):
    slots, rows = stage_ref.shape[0], stage_ref.shape[1]
    n = src_hbm.shape[0] // rows

    def copy(k):
        slot = k % slots
        return pltpu.make_async_copy(src_hbm.at[pl.ds(k * rows, rows), :], stage_ref.at[slot], sem_ref.at[slot])

    for k in range(min(slots, n)):
        copy(k).start()
    for k in range(n):
        copy(k).wait()
        dst_ref[k * rows:(k + 1) * rows, :] = stage_ref[k % slots].astype(BF16)
        if k + slots < n:
            copy(k + slots).start()


def _mixer_kernel(cd_ref, x_ref, g1_ref, win_hbm, convw_ref, cng_ref, rng_ref, wout_hbm,
                  cos_ref, sin_ref, decay_ref, zeta_ref, xi_ref, wup_ref, wdn_ref,
                  h_ref, wup_bf_ref, wdn_bf_ref,
                  win_ref, wout_ref, win_stage_ref, wout_stage_ref, cast_sem_ref,
                  state_ref, pd1_ref, pd2_ref, mix_ref, lhs_ref, rhs_ref):
    tile = x_ref.shape[0]

    @pl.when((pl.program_id(0) == 0) & (pl.program_id(1) == 0))
    def _():
        _cast_rows_to_bf16(win_hbm, win_ref, win_stage_ref, cast_sem_ref)
        _cast_rows_to_bf16(wout_hbm, wout_ref, wout_stage_ref, cast_sem_ref)

    wup_bf_ref[...] = wup_ref[...].astype(BF16)
    wdn_bf_ref[...] = wdn_ref[...].astype(BF16)
    n_chunks = tile // CHUNK
    units = [(c, hd) for c in range(n_chunks) for hd in range(RET_HEADS)]

    @pl.when(pl.program_id(1) == 0)
    def _():
        state_ref[...] = jnp.zeros_like(state_ref)
        zeros = jnp.zeros((SUBLANES, CONV_WIDTH), F32)
        pd1_ref[0:SUBLANES, :] = zeros
        pd2_ref[0:SUBLANES, :] = zeros
        pd1_ref[tile:tile + SUBLANES, :] = zeros
        pd2_ref[tile:tile + SUBLANES, :] = zeros

    row_blocks = [slice(r * PROJ_ROWS, (r + 1) * PROJ_ROWS) for r in range(tile // PROJ_ROWS)]
    us = []
    for rb in row_blocks:
        xb = x_ref[rb, :]
        us.append((xb * _rms_scale(xb) * g1_ref[...]).astype(BF16))

    def in_proj(col0, width):
        w = win_ref[:, col0:col0 + width]
        return jnp.concatenate([jnp.dot(ub, w, preferred_element_type=F32) for ub in us], axis=0)

    c0 = 3 * CONV_WIDTH

    zq = in_proj(c0, RET_WIDTH)
    zk = in_proj(c0 + RET_WIDTH, RET_WIDTH)
    cb = in_proj(0, CONV_WIDTH)
    cc = in_proj(CONV_WIDTH, CONV_WIDTH)

    cos = cos_ref[...]
    sin = sin_ref[...]
    q_rot, kt_rot = [], []
    for hd in range(RET_HEADS):
        hs = slice(hd * HEAD_DIM, (hd + 1) * HEAD_DIM)
        qh = zq[:, hs]
        kh = zk[:, hs]
        q_rot.append(qh * cos + pltpu.roll(qh, HEAD_DIM // 2, axis=1) * sin)
        kt_rot.append((kh * cos + pltpu.roll(kh, HEAD_DIM // 2, axis=1) * sin).T)

    for n, (c, hd) in enumerate(units):
        rows = slice(c * CHUNK, (c + 1) * CHUNK)
        q = q_rot[hd][rows]
        kt = kt_rot[hd][:, rows]
        s = jnp.dot(q.astype(BF16), kt.astype(BF16), preferred_element_type=F32) * decay_ref[hd]
        lhs_ref[n, :, 0:CHUNK] = s.astype(BF16)
        lhs_ref[n, :, CHUNK:2 * CHUNK] = (q * xi_ref[hd]).astype(BF16)

    ch = in_proj(2 * CONV_WIDTH, CONV_WIDTH)
    zv = in_proj(c0 + 2 * RET_WIDTH, RET_WIDTH)
    v_bf = zv.astype(BF16)

    states = [state_ref[hd] for hd in range(RET_HEADS)]
    for n, (c, hd) in enumerate(units):
        rows = slice(c * CHUNK, (c + 1) * CHUNK)
        v = v_bf[rows, hd * HEAD_DIM:(hd + 1) * HEAD_DIM]
        kzt = (kt_rot[hd][:, rows] * zeta_ref[hd]).astype(BF16)
        kv = jnp.dot(kzt, v, preferred_element_type=F32)
        rhs_ref[n, 0:CHUNK, :] = v
        rhs_ref[n, CHUNK:2 * CHUNK, :] = states[hd].astype(BF16)
        states[hd] = cd_ref[hd] * states[hd] + kv
    for hd in range(RET_HEADS):
        state_ref[hd] = states[hd]

    zg = in_proj(c0 + 3 * RET_WIDTH, RET_WIDTH)

    p = cc * ch
    pd1_ref[1:1 + tile, :] = p
    pd2_ref[2:2 + tile, :] = p
    p1 = pd1_ref[0:tile, :]
    p2 = pd2_ref[0:tile, :]
    y = cb * (p2 * convw_ref[0:1, :] + p1 * convw_ref[1:2, :] + p * convw_ref[2:3, :])
    pd1_ref[0:SUBLANES, :] = pd1_ref[tile:tile + SUBLANES, :]
    pd2_ref[0:SUBLANES, :] = pd2_ref[tile:tile + SUBLANES, :]

    lane = lax.broadcasted_iota(jnp.int32, (tile, LANES), 1)
    low = lane < CONV_GROUP_DIM
    for blk in range(CONV_WIDTH // LANES):
        sl = slice(blk * LANES, (blk + 1) * LANES)
        yb = y[:, sl]
        y2 = yb * yb
        ss_lo = jnp.sum(jnp.where(low, y2, 0.0), axis=-1, keepdims=True)
        ss_hi = jnp.sum(jnp.where(low, 0.0, y2), axis=-1, keepdims=True)
        inv = lax.rsqrt(jnp.where(low, ss_lo, ss_hi) * (1.0 / CONV_GROUP_DIM) + NORM_EPS)
        mix_ref[:, sl] = (yb * inv * cng_ref[:, sl]).astype(BF16)

    outs = [jnp.dot(lhs_ref[n], rhs_ref[n], preferred_element_type=F32) for n in range(len(units))]

    accs = [jnp.dot(mix_ref[rb, 0:CONV_WIDTH], wout_ref[0:CONV_WIDTH, :], preferred_element_type=F32)
            for rb in row_blocks]

    for hd in range(RET_HEADS):
        hs = slice(hd * HEAD_DIM, (hd + 1) * HEAD_DIM)
        o = jnp.concatenate([outs[c * RET_HEADS + hd] for c in range(n_chunks)], axis=0)
        gate = zg[:, hs]
        gate = gate * (1.0 / (1.0 + jnp.exp(-gate)))
        yr = o * _rms_scale(o) * rng_ref[:, hs] * gate
        mix_ref[:, CONV_WIDTH + hd * HEAD_DIM:CONV_WIDTH + (hd + 1) * HEAD_DIM] = yr.astype(BF16)

    for rb, acc in zip(row_blocks, accs):
        h_ref[rb, :] = x_ref[rb, :] + acc + jnp.dot(mix_ref[rb, CONV_WIDTH:], wout_ref[CONV_WIDTH:, :],
                                                    preferred_element_type=F32)


def _mlp_kernel(h_ref, g2_ref, wup_ref, wdn_ref, gf_ref, o_ref, hid_ref):
    for r in range(h_ref.shape[0] // MLP_SUB):
        rows = slice(r * MLP_SUB, (r + 1) * MLP_SUB)
        h = h_ref[rows, :]
        u = (h * _rms_scale(h) * g2_ref[...]).astype(BF16)
        for c in range(D_FF // FF_CHUNK):
            cols = slice(c * FF_CHUNK, (c + 1) * FF_CHUNK)
            a = jnp.maximum(_dot(u, wup_ref[:, cols]), 0.0)
            hid_ref[rows, cols] = (a * a).astype(BF16)
        y = h + _dot(hid_ref[rows, :], wdn_ref[...])
        o_ref[rows, :] = y * _rms_scale(y) * gf_ref[...]


def _resident(shape):
    nd = len(shape)
    return pl.BlockSpec(shape, lambda *_: (0,) * nd, pipeline_mode=pl.Buffered(1))


@functools.lru_cache(maxsize=None)
def _retention_tables(seq):
    half = HEAD_DIM // 2
    inv_freq = 1.0 / (ROPE_BASE ** (np.arange(half, dtype=np.float64) / half))
    ang = np.arange(seq, dtype=np.float64)[:, None] * inv_freq[None, :]
    cos = np.cos(ang)
    sin = np.sin(ang)
    cos_t = np.concatenate([cos, cos], axis=-1)
    sin_t = np.concatenate([-sin, sin], axis=-1)

    log_gamma = np.log(1.0 - 2.0 ** (-5.0 - np.arange(RET_HEADS, dtype=np.float64)))
    idx = np.arange(CHUNK, dtype=np.float64)
    diff = idx[:, None] - idx[None, :]
    intra = np.where(diff[None] >= 0, np.exp(log_gamma[:, None, None] * np.maximum(diff, 0.0)[None]), 0.0)
    zeta = np.exp(log_gamma[:, None] * (CHUNK - 1 - idx)[None])
    xi = np.exp(log_gamma[:, None] * (idx + 1.0)[None])
    chunk_decay = np.exp(log_gamma * CHUNK)
    k_scale = HEAD_DIM ** -0.5
    decay_t = intra * k_scale
    zeta_t = np.broadcast_to((zeta * k_scale)[:, None, :], (RET_HEADS, HEAD_DIM, CHUNK))
    xi_t = np.broadcast_to(xi[:, :, None], (RET_HEADS, CHUNK, HEAD_DIM))
    return tuple(np.ascontiguousarray(t, dtype=np.float32)
                 for t in (cos_t, sin_t, decay_t, zeta_t, xi_t, chunk_decay))


def kernel(x, norm1_g, w_in, conv_w, conv_norm_g, ret_norm_g, w_out, norm2_g, w_up, w_down, final_norm_g):
    batch, seq, d_model = x.shape
    assert d_model == D_MODEL and w_in.shape == (D_MODEL, IN_COLS)
    assert seq % MIXER_TILE == 0 and MIXER_TILE % CHUNK == 0 and (batch * seq) % MLP_TILE == 0

    cos_t, sin_t, decay_t, zeta_t, xi_t, chunk_decay = _retention_tables(seq)
    row = lambda g: g.reshape(1, -1).astype(F32)
    n_units = (MIXER_TILE // CHUNK) * RET_HEADS

    tile_spec = pl.BlockSpec((None, MIXER_TILE, D_MODEL), lambda b, j: (b, j, 0))
    rope_spec = pl.BlockSpec((MIXER_TILE, HEAD_DIM), lambda b, j: (j, 0))
    seq_tiles = seq // MIXER_TILE
    n_steps = batch * seq_tiles
    assert D_MODEL % n_steps == 0 and D_FF % n_steps == 0
    wup_slab = pl.BlockSpec((D_MODEL // n_steps, D_FF), lambda b, j: (b * seq_tiles + j, 0))
    wdn_slab = pl.BlockSpec((D_FF // n_steps, D_MODEL), lambda b, j: (b * seq_tiles + j, 0))
    h, w_up_bf, w_down_bf = pl.pallas_call(
        _mixer_kernel,
        grid=(batch, seq_tiles),
        in_specs=[
            pl.BlockSpec(memory_space=pltpu.SMEM),
            tile_spec,
            _resident((1, D_MODEL)),
            pl.BlockSpec(memory_space=pl.ANY),
            _resident((CONV_K, CONV_WIDTH)),
            _resident((1, CONV_WIDTH)),
            _resident((1, RET_WIDTH)),
            pl.BlockSpec(memory_space=pl.ANY),
            rope_spec,
            rope_spec,
            _resident((RET_HEADS, CHUNK, CHUNK)),
            _resident((RET_HEADS, CHUNK, HEAD_DIM)),
            _resident((RET_HEADS, CHUNK, HEAD_DIM)),
            wup_slab,
            wdn_slab,
        ],
        out_specs=[tile_spec, wup_slab, wdn_slab],
        out_shape=[jax.ShapeDtypeStruct(x.shape, F32),
                   jax.ShapeDtypeStruct(w_up.shape, BF16),
                   jax.ShapeDtypeStruct(w_down.shape, BF16)],
        scratch_shapes=[
            pltpu.VMEM((D_MODEL, IN_COLS), BF16),
            pltpu.VMEM((D_MODEL, D_MODEL), BF16),
            pltpu.VMEM((CAST_SLOTS, CAST_ROWS, IN_COLS), F32),
            pltpu.VMEM((CAST_SLOTS, CAST_ROWS, D_MODEL), F32),
            pltpu.SemaphoreType.DMA((CAST_SLOTS,)),
            pltpu.VMEM((RET_HEADS, HEAD_DIM, HEAD_DIM), F32),
            pltpu.VMEM((MIXER_TILE + SUBLANES, CONV_WIDTH), F32),
            pltpu.VMEM((MIXER_TILE + SUBLANES, CONV_WIDTH), F32),
            pltpu.VMEM((MIXER_TILE, D_MODEL), BF16),
            pltpu.VMEM((n_units, CHUNK, 2 * CHUNK), BF16),
            pltpu.VMEM((n_units, 2 * CHUNK, HEAD_DIM), BF16),
        ],
        compiler_params=pltpu.CompilerParams(
            dimension_semantics=("arbitrary", "arbitrary"),
            vmem_limit_bytes=VMEM_LIMIT_BYTES),
        name="mixer",
    )(chunk_decay, x, row(norm1_g), w_in, conv_w, row(conv_norm_g), row(ret_norm_g),
      w_out, cos_t, sin_t, decay_t, zeta_t, xi_t, w_up, w_down)

    tokens = batch * seq
    tok_spec = pl.BlockSpec((MLP_TILE, D_MODEL), lambda i: (i, 0))
    out = pl.pallas_call(
        _mlp_kernel,
        grid=(tokens // MLP_TILE,),
        in_specs=[
            tok_spec,
            _resident((1, D_MODEL)),
            _resident((D_MODEL, D_FF)),
            _resident((D_FF, D_MODEL)),
            _resident((1, D_MODEL)),
        ],
        out_specs=tok_spec,
        out_shape=jax.ShapeDtypeStruct((tokens, D_MODEL), F32),
        scratch_shapes=[pltpu.VMEM((MLP_TILE, D_FF), BF16)],
        compiler_params=pltpu.CompilerParams(
            dimension_semantics=("arbitrary",),
            vmem_limit_bytes=VMEM_LIMIT_BYTES),
        name="mlp",
    )(h.reshape(tokens, D_MODEL), row(norm2_g), w_up_bf, w_down_bf, row(final_norm_g))
    return out.reshape(batch, seq, D_MODEL)
```

```python
import functools

import jax
import jax.numpy as jnp
import numpy as np
from jax import lax
from jax.experimental import pallas as pl
from jax.experimental.pallas import tpu as pltpu

D_MODEL = 1024
CONV_WIDTH = 512
CONV_GROUPS = 8
CONV_GROUP_DIM = CONV_WIDTH // CONV_GROUPS
CONV_K = 3
RET_WIDTH = 512
RET_HEADS = 4
HEAD_DIM = RET_WIDTH // RET_HEADS
CHUNK = 128
ROPE_BASE = 10000.0
D_FF = 4 * D_MODEL
NORM_EPS = 1e-6
IN_COLS = 3 * CONV_WIDTH + 4 * RET_WIDTH

LANES = 128
SUBLANES = 8
VMEM_LIMIT_BYTES = 56 * 1024 * 1024

MIXER_TILE = 1024
MLP_TILE = 1024
PROJ_ROWS = 1024
MLP_DOWN_ROWS = 256
FF_CHUNK = 1024
CAST_ROWS = 32
CAST_SLOTS = 8

BF16 = jnp.bfloat16
F32 = jnp.float32


def _dot(a, b):
    return jnp.dot(a, b, preferred_element_type=F32)


def _rms_scale(x):
    return lax.rsqrt(jnp.mean(x * x, axis=-1, keepdims=True) + NORM_EPS)


def _cast_rows_to_bf16(src_hbm, dst_ref, stage_ref, sem_ref):
    slots, rows = stage_ref.shape[0], stage_ref.shape[1]
    n = src_hbm.shape[0] // rows

    def copy(k):
        slot = k % slots
        return pltpu.make_async_copy(src_hbm.at[pl.ds(k * rows, rows), :], stage_ref.at[slot], sem_ref.at[slot])

    for k in range(min(slots, n)):
        copy(k).start()
    for k in range(n):
        copy(k).wait()
        dst_ref[k * rows:(k + 1) * rows, :] = stage_ref[k % slots].astype(BF16)
        if k + slots < n:
            copy(k + slots).start()


def _mixer_kernel(cd_ref, x_ref, g1_ref, win_hbm, convw_ref, cng_ref, rng_ref, wout_hbm,
                  cos_ref, sin_ref, decay_ref, zeta_ref, xi_ref, wup_ref, wdn_ref,
                  h_ref, wup_bf_ref, wdn_bf_ref,
                  win_ref, wout_ref, win_stage_ref, wout_stage_ref, cast_sem_ref,
                  state_ref, pd1_ref, pd2_ref, mix_ref, lhs_ref, rhs_ref):
    tile = x_ref.shape[0]

    @pl.when((pl.program_id(0) == 0) & (pl.program_id(1) == 0))
    def _():
        _cast_rows_to_bf16(win_hbm, win_ref, win_stage_ref, cast_sem_ref)
        _cast_rows_to_bf16(wout_hbm, wout_ref, wout_stage_ref, cast_sem_ref)

    wup_bf_ref[...] = wup_ref[...].astype(BF16)
    wdn_bf_ref[...] = wdn_ref[...].astype(BF16)
    n_chunks = tile // CHUNK
    units = [(c, hd) for c in range(n_chunks) for hd in range(RET_HEADS)]

    @pl.when(pl.program_id(1) == 0)
    def _():
        state_ref[...] = jnp.zeros_like(state_ref)
        zeros = jnp.zeros((SUBLANES, CONV_WIDTH), F32)
        pd1_ref[0:SUBLANES, 0:CONV_WIDTH] = zeros
        pd2_ref[0:SUBLANES, 0:CONV_WIDTH] = zeros
        pd1_ref[tile:tile + SUBLANES, 0:CONV_WIDTH] = zeros
        pd2_ref[tile:tile + SUBLANES, 0:CONV_WIDTH] = zeros

    row_blocks = [slice(r * PROJ_ROWS, (r + 1) * PROJ_ROWS) for r in range(tile // PROJ_ROWS)]
    us = []
    for rb in row_blocks:
        xb = x_ref[rb, :]
        us.append((xb * _rms_scale(xb) * g1_ref[...]).astype(BF16))

    def in_proj(col0, width):
        w = win_ref[:, col0:col0 + width]
        return jnp.concatenate([jnp.dot(ub, w, preferred_element_type=F32) for ub in us], axis=0)

    c0 = 3 * CONV_WIDTH

    zq = in_proj(c0, RET_WIDTH)
    zk = in_proj(c0 + RET_WIDTH, RET_WIDTH)
    cb = in_proj(0, CONV_WIDTH)
    cc = in_proj(CONV_WIDTH, CONV_WIDTH)

    cos = cos_ref[...]
    sin = sin_ref[...]
    q_rot, kt_rot = [], []
    for hd in range(RET_HEADS):
        hs = slice(hd * HEAD_DIM, (hd + 1) * HEAD_DIM)
        qh = zq[:, hs]
        kh = zk[:, hs]
        q_rot.append(qh * cos + pltpu.roll(qh, HEAD_DIM // 2, axis=1) * sin)
        kt_rot.append((kh * cos + pltpu.roll(kh, HEAD_DIM // 2, axis=1) * sin).T)

    for n, (c, hd) in enumerate(units):
        rows = slice(c * CHUNK, (c + 1) * CHUNK)
        q = q_rot[hd][rows]
        kt = kt_rot[hd][:, rows]
        s = jnp.dot(q.astype(BF16), kt.astype(BF16), preferred_element_type=F32) * decay_ref[hd]
        lhs_ref[n, :, 0:CHUNK] = s.astype(BF16)
        lhs_ref[n, :, CHUNK:2 * CHUNK] = (q * xi_ref[hd]).astype(BF16)

    ch = in_proj(2 * CONV_WIDTH, CONV_WIDTH)
    zv = in_proj(c0 + 2 * RET_WIDTH, RET_WIDTH)
    v_bf = zv.astype(BF16)

    states = [state_ref[hd] for hd in range(RET_HEADS)]
    for n, (c, hd) in enumerate(units):
        rows = slice(c * CHUNK, (c + 1) * CHUNK)
        v = v_bf[rows, hd * HEAD_DIM:(hd + 1) * HEAD_DIM]
        kzt = (kt_rot[hd][:, rows] * zeta_ref[hd]).astype(BF16)
        kv = jnp.dot(kzt, v, preferred_element_type=F32)
        rhs_ref[n, 0:CHUNK, :] = v
        rhs_ref[n, CHUNK:2 * CHUNK, :] = states[hd].astype(BF16)
        states[hd] = cd_ref[hd] * states[hd] + kv
    for hd in range(RET_HEADS):
        state_ref[hd] = states[hd]

    zg = in_proj(c0 + 3 * RET_WIDTH, RET_WIDTH)

    p = cc * ch
    pd1_ref[1:1 + tile, 0:CONV_WIDTH] = p
    pd2_ref[2:2 + tile, 0:CONV_WIDTH] = p
    p1 = pd1_ref[0:tile, 0:CONV_WIDTH]
    p2 = pd2_ref[0:tile, 0:CONV_WIDTH]
    y = cb * (p2 * convw_ref[0:1, :] + p1 * convw_ref[1:2, :] + p * convw_ref[2:3, :])
    pd1_ref[0:SUBLANES, 0:CONV_WIDTH] = pd1_ref[tile:tile + SUBLANES, 0:CONV_WIDTH]
    pd2_ref[0:SUBLANES, 0:CONV_WIDTH] = pd2_ref[tile:tile + SUBLANES, 0:CONV_WIDTH]

    lane = lax.broadcasted_iota(jnp.int32, (tile, LANES), 1)
    low = lane < CONV_GROUP_DIM
    for blk in range(CONV_WIDTH // LANES):
        sl = slice(blk * LANES, (blk + 1) * LANES)
        yb = y[:, sl]
        y2 = yb * yb
        ss_lo = jnp.sum(jnp.where(low, y2, 0.0), axis=-1, keepdims=True)
        ss_hi = jnp.sum(jnp.where(low, 0.0, y2), axis=-1, keepdims=True)
        inv = lax.rsqrt(jnp.where(low, ss_lo, ss_hi) * (1.0 / CONV_GROUP_DIM) + NORM_EPS)
        mix_ref[:, sl] = (yb * inv * cng_ref[:, sl]).astype(BF16)

    outs = [jnp.dot(lhs_ref[n], rhs_ref[n], preferred_element_type=F32) for n in range(len(units))]

    accs = [jnp.dot(mix_ref[rb, 0:CONV_WIDTH], wout_ref[0:CONV_WIDTH, :], preferred_element_type=F32)
            for rb in row_blocks]

    for hd in range(RET_HEADS):
        hs = slice(hd * HEAD_DIM, (hd + 1) * HEAD_DIM)
        o = jnp.concatenate([outs[c * RET_HEADS + hd] for c in range(n_chunks)], axis=0)
        gate = zg[:, hs]
        gate = gate * (1.0 / (1.0 + jnp.exp(-gate)))
        yr = o * _rms_scale(o) * rng_ref[:, hs] * gate
        mix_ref[:, CONV_WIDTH + hd * HEAD_DIM:CONV_WIDTH + (hd + 1) * HEAD_DIM] = yr.astype(BF16)

    for rb, acc in zip(row_blocks, accs):
        h_ref[rb, :] = x_ref[rb, :] + acc + jnp.dot(mix_ref[rb, CONV_WIDTH:D_MODEL], wout_ref[CONV_WIDTH:, :],
                                                    preferred_element_type=F32)


def _mlp_kernel(h_ref, g2_ref, wup_ref, wdn_ref, gf_ref, o_ref, hid_ref):
    h = h_ref[...]
    u = (h * _rms_scale(h) * g2_ref[...]).astype(BF16)
    for c in range(D_FF // FF_CHUNK):
        cols = slice(c * FF_CHUNK, (c + 1) * FF_CHUNK)
        a = jnp.maximum(_dot(u, wup_ref[:, cols]), 0.0)
        hid_ref[:, cols] = (a * a).astype(BF16)
    for r in range(h_ref.shape[0] // MLP_DOWN_ROWS):
        rows = slice(r * MLP_DOWN_ROWS, (r + 1) * MLP_DOWN_ROWS)
        y = h_ref[rows, :] + _dot(hid_ref[rows, :], wdn_ref[...])
        o_ref[rows, :] = y * _rms_scale(y) * gf_ref[...]


def _resident(shape):
    nd = len(shape)
    return pl.BlockSpec(shape, lambda *_: (0,) * nd, pipeline_mode=pl.Buffered(1))


@functools.lru_cache(maxsize=None)
def _retention_tables(seq):
    half = HEAD_DIM // 2
    inv_freq = 1.0 / (ROPE_BASE ** (np.arange(half, dtype=np.float64) / half))
    ang = np.arange(seq, dtype=np.float64)[:, None] * inv_freq[None, :]
    cos = np.cos(ang)
    sin = np.sin(ang)
    cos_t = np.concatenate([cos, cos], axis=-1)
    sin_t = np.concatenate([-sin, sin], axis=-1)

    log_gamma = np.log(1.0 - 2.0 ** (-5.0 - np.arange(RET_HEADS, dtype=np.float64)))
    idx = np.arange(CHUNK, dtype=np.float64)
    diff = idx[:, None] - idx[None, :]
    intra = np.where(diff[None] >= 0, np.exp(log_gamma[:, None, None] * np.maximum(diff, 0.0)[None]), 0.0)
    zeta = np.exp(log_gamma[:, None] * (CHUNK - 1 - idx)[None])
    xi = np.exp(log_gamma[:, None] * (idx + 1.0)[None])
    chunk_decay = np.exp(log_gamma * CHUNK)
    k_scale = HEAD_DIM ** -0.5
    decay_t = intra * k_scale
    zeta_t = np.broadcast_to((zeta * k_scale)[:, None, :], (RET_HEADS, HEAD_DIM, CHUNK))
    xi_t = np.broadcast_to(xi[:, :, None], (RET_HEADS, CHUNK, HEAD_DIM))
    return tuple(np.ascontiguousarray(t, dtype=np.float32)
                 for t in (cos_t, sin_t, decay_t, zeta_t, xi_t, chunk_decay))


def kernel(x, norm1_g, w_in, conv_w, conv_norm_g, ret_norm_g, w_out, norm2_g, w_up, w_down, final_norm_g):
    batch, seq, d_model = x.shape
    assert d_model == D_MODEL and w_in.shape == (D_MODEL, IN_COLS)
    assert seq % MIXER_TILE == 0 and MIXER_TILE % CHUNK == 0 and (batch * seq) % MLP_TILE == 0

    cos_t, sin_t, decay_t, zeta_t, xi_t, chunk_decay = _retention_tables(seq)
    row = lambda g: g.reshape(1, -1).astype(F32)
    n_units = (MIXER_TILE // CHUNK) * RET_HEADS

    tile_spec = pl.BlockSpec((None, MIXER_TILE, D_MODEL), lambda b, j: (b, j, 0))
    rope_spec = pl.BlockSpec((MIXER_TILE, HEAD_DIM), lambda b, j: (j, 0))
    seq_tiles = seq // MIXER_TILE
    n_steps = batch * seq_tiles
    assert D_MODEL % n_steps == 0 and D_FF % n_steps == 0
    wup_slab = pl.BlockSpec((D_MODEL // n_steps, D_FF), lambda b, j: (b * seq_tiles + j, 0))
    wdn_slab = pl.BlockSpec((D_FF // n_steps, D_MODEL), lambda b, j: (b * seq_tiles + j, 0))
    h, w_up_bf, w_down_bf = pl.pallas_call(
        _mixer_kernel,
        grid=(batch, seq_tiles),
        in_specs=[
            pl.BlockSpec(memory_space=pltpu.SMEM),
            tile_spec,
            _resident((1, D_MODEL)),
            pl.BlockSpec(memory_space=pl.ANY),
            _resident((CONV_K, CONV_WIDTH)),
            _resident((1, CONV_WIDTH)),
            _resident((1, RET_WIDTH)),
            pl.BlockSpec(memory_space=pl.ANY),
            rope_spec,
            rope_spec,
            _resident((RET_HEADS, CHUNK, CHUNK)),
            _resident((RET_HEADS, CHUNK, HEAD_DIM)),
            _resident((RET_HEADS, CHUNK, HEAD_DIM)),
            wup_slab,
            wdn_slab,
        ],
        out_specs=[tile_spec, wup_slab, wdn_slab],
        out_shape=[jax.ShapeDtypeStruct(x.shape, F32),
                   jax.ShapeDtypeStruct(w_up.shape, BF16),
                   jax.ShapeDtypeStruct(w_down.shape, BF16)],
        scratch_shapes=[
            pltpu.VMEM((D_MODEL, IN_COLS), BF16),
            pltpu.VMEM((D_MODEL, D_MODEL), BF16),
            pltpu.VMEM((CAST_SLOTS, CAST_ROWS, IN_COLS), F32),
            pltpu.VMEM((CAST_SLOTS, CAST_ROWS, D_MODEL), F32),
            pltpu.SemaphoreType.DMA((CAST_SLOTS,)),
            pltpu.VMEM((RET_HEADS, HEAD_DIM, HEAD_DIM), F32),
            pltpu.VMEM((MIXER_TILE + SUBLANES, CONV_WIDTH), F32),
            pltpu.VMEM((MIXER_TILE + SUBLANES, CONV_WIDTH), F32),
            pltpu.VMEM((MIXER_TILE, D_MODEL), BF16),
            pltpu.VMEM((n_units, CHUNK, 2 * CHUNK), BF16),
            pltpu.VMEM((n_units, 2 * CHUNK, HEAD_DIM), BF16),
        ],
        compiler_params=pltpu.CompilerParams(
            dimension_semantics=("arbitrary", "arbitrary"),
            vmem_limit_bytes=VMEM_LIMIT_BYTES),
        name="mixer",
    )(chunk_decay, x, row(norm1_g), w_in, conv_w, row(conv_norm_g), row(ret_norm_g),
      w_out, cos_t, sin_t, decay_t, zeta_t, xi_t, w_up, w_down)

    tokens = batch * seq
    tok_spec = pl.BlockSpec((MLP_TILE, D_MODEL), lambda i: (i, 0))
    out = pl.pallas_call(
        _mlp_kernel,
        grid=(tokens // MLP_TILE,),
        in_specs=[
            tok_spec,
            _resident((1, D_MODEL)),
            _resident((D_MODEL, D_FF)),
            _resident((D_FF, D_MODEL)),
            _resident((1, D_MODEL)),
        ],
        out_specs=tok_spec,
        out_shape=jax.ShapeDtypeStruct((tokens, D_MODEL), F32),
        scratch_shapes=[pltpu.VMEM((MLP_TILE, D_FF), BF16)],
        compiler_params=pltpu.CompilerParams(
            dimension_semantics=("arbitrary",),
            vmem_limit_bytes=VMEM_LIMIT_BYTES),
        name="mlp",
    )(h.reshape(tokens, D_MODEL), row(norm2_g), w_up_bf, w_down_bf, row(final_norm_g))
    return out.reshape(batch, seq, D_MODEL)
```

```python
import functools

import jax
import jax.numpy as jnp
import numpy as np
from jax import lax
from jax.experimental import pallas as pl
from jax.experimental.pallas import tpu as pltpu

D_MODEL = 1024
CONV_WIDTH = 512
CONV_GROUPS = 8
CONV_GROUP_DIM = CONV_WIDTH // CONV_GROUPS
CONV_K = 3
RET_WIDTH = 512
RET_HEADS = 4
HEAD_DIM = RET_WIDTH // RET_HEADS
CHUNK = 128
ROPE_BASE = 10000.0
D_FF = 4 * D_MODEL
NORM_EPS = 1e-6
IN_COLS = 3 * CONV_WIDTH + 4 * RET_WIDTH

LANES = 128
SUBLANES = 8
VMEM_LIMIT_BYTES = 56 * 1024 * 1024

MIXER_TILE = 1024
MLP_TILE = 1024
PROJ_ROWS = 1024
MLP_DOWN_ROWS = 256
FF_CHUNK = 1024
CAST_ROWS = 16
CAST_SLOTS = 8

BF16 = jnp.bfloat16
F32 = jnp.float32


def _dot(a, b):
    return jnp.dot(a, b, preferred_element_type=F32)


def _rms_scale(x):
    return lax.rsqrt(jnp.mean(x * x, axis=-1, keepdims=True) + NORM_EPS)


def _cast_rows_to_bf16(src_hbm, dst_ref, stage_ref, sem_ref):
    slots, rows = stage_ref.shape[0], stage_ref.shape[1]
    n = src_hbm.shape[0] // rows

    def copy(k):
        slot = k % slots
        return pltpu.make_async_copy(src_hbm.at[pl.ds(k * rows, rows), :], stage_ref.at[slot], sem_ref.at[slot])

    for k in range(min(slots, n)):
        copy(k).start()
    for k in range(n):
        copy(k).wait()
        dst_ref[k * rows:(k + 1) * rows, :] = stage_ref[k % slots].astype(BF16)
        if k + slots < n:
            copy(k + slots).start()


def _mixer_kernel(cd_ref, x_ref, g1_ref, win_hbm, convw_ref, cng_ref, rng_ref, wout_hbm,
                  cos_ref, sin_ref, decay_ref, zeta_ref, xi_ref, wup_ref, wdn_ref,
                  h_ref, wup_bf_ref, wdn_bf_ref,
                  win_ref, wout_ref, wkt_ref, win_stage_ref, wout_stage_ref, cast_sem_ref,
                  state_ref, pd1_ref, pd2_ref, mix_ref, lhs_ref, rhs_ref):
    tile = x_ref.shape[0]

    @pl.when((pl.program_id(0) == 0) & (pl.program_id(1) == 0))
    def _():
        _cast_rows_to_bf16(win_hbm, win_ref, win_stage_ref, cast_sem_ref)
        _cast_rows_to_bf16(wout_hbm, wout_ref, wout_stage_ref, cast_sem_ref)
        k0 = 3 * CONV_WIDTH + RET_WIDTH
        wkt_ref[...] = win_ref[:, k0:k0 + RET_WIDTH].astype(F32).T.astype(BF16)

    wup_bf_ref[...] = wup_ref[...].astype(BF16)
    wdn_bf_ref[...] = wdn_ref[...].astype(BF16)
    n_chunks = tile // CHUNK
    units = [(c, hd) for c in range(n_chunks) for hd in range(RET_HEADS)]

    @pl.when(pl.program_id(1) == 0)
    def _():
        state_ref[...] = jnp.zeros_like(state_ref)
        zeros = jnp.zeros((SUBLANES, CONV_WIDTH), F32)
        pd1_ref[0:SUBLANES, 0:CONV_WIDTH] = zeros
        pd2_ref[0:SUBLANES, 0:CONV_WIDTH] = zeros
        pd1_ref[tile:tile + SUBLANES, 0:CONV_WIDTH] = zeros
        pd2_ref[tile:tile + SUBLANES, 0:CONV_WIDTH] = zeros

    row_blocks = [slice(r * PROJ_ROWS, (r + 1) * PROJ_ROWS) for r in range(tile // PROJ_ROWS)]
    us = []
    for rb in row_blocks:
        xb = x_ref[rb, :]
        us.append((xb * _rms_scale(xb) * g1_ref[...]).astype(BF16))

    def in_proj(col0, width):
        w = win_ref[:, col0:col0 + width]
        return jnp.concatenate([jnp.dot(ub, w, preferred_element_type=F32) for ub in us], axis=0)

    c0 = 3 * CONV_WIDTH

    zq = in_proj(c0, RET_WIDTH)
    zkt = lax.dot_general(wkt_ref[...], jnp.concatenate(us, axis=0), (((1,), (1,)), ((), ())),
                          preferred_element_type=F32)
    cb = in_proj(0, CONV_WIDTH)
    cc = in_proj(CONV_WIDTH, CONV_WIDTH)

    cos = cos_ref[...]
    sin = sin_ref[...]
    cos_t = cos.T
    sin_t = sin.T
    half = HEAD_DIM // 2
    q_rot, kt_rot = [], []
    for hd in range(RET_HEADS):
        hs = slice(hd * HEAD_DIM, (hd + 1) * HEAD_DIM)
        qh = zq[:, hs]
        q_rot.append(qh * cos + pltpu.roll(qh, half, axis=1) * sin)
        kth = zkt[hs, :]
        kth_swapped = jnp.concatenate([kth[half:, :], kth[:half, :]], axis=0)
        kt_rot.append(kth * cos_t + kth_swapped * sin_t)

    for n, (c, hd) in enumerate(units):
        rows = slice(c * CHUNK, (c + 1) * CHUNK)
        q = q_rot[hd][rows]
        kt = kt_rot[hd][:, rows]
        s = jnp.dot(q.astype(BF16), kt.astype(BF16), preferred_element_type=F32) * decay_ref[hd]
        lhs_ref[n, :, 0:CHUNK] = s.astype(BF16)
        lhs_ref[n, :, CHUNK:2 * CHUNK] = (q * xi_ref[hd]).astype(BF16)

    ch = in_proj(2 * CONV_WIDTH, CONV_WIDTH)
    zv = in_proj(c0 + 2 * RET_WIDTH, RET_WIDTH)
    v_bf = zv.astype(BF16)

    states = [state_ref[hd] for hd in range(RET_HEADS)]
    for n, (c, hd) in enumerate(units):
        rows = slice(c * CHUNK, (c + 1) * CHUNK)
        v = v_bf[rows, hd * HEAD_DIM:(hd + 1) * HEAD_DIM]
        kzt = (kt_rot[hd][:, rows] * zeta_ref[hd]).astype(BF16)
        kv = jnp.dot(kzt, v, preferred_element_type=F32)
        rhs_ref[n, 0:CHUNK, :] = v
        rhs_ref[n, CHUNK:2 * CHUNK, :] = states[hd].astype(BF16)
        states[hd] = cd_ref[hd] * states[hd] + kv
    for hd in range(RET_HEADS):
        state_ref[hd] = states[hd]

    zg = in_proj(c0 + 3 * RET_WIDTH, RET_WIDTH)

    p = cc * ch
    pd1_ref[1:1 + tile, 0:CONV_WIDTH] = p
    pd2_ref[2:2 + tile, 0:CONV_WIDTH] = p
    p1 = pd1_ref[0:tile, 0:CONV_WIDTH]
    p2 = pd2_ref[0:tile, 0:CONV_WIDTH]
    y = cb * (p2 * convw_ref[0:1, :] + p1 * convw_ref[1:2, :] + p * convw_ref[2:3, :])
    pd1_ref[0:SUBLANES, 0:CONV_WIDTH] = pd1_ref[tile:tile + SUBLANES, 0:CONV_WIDTH]
    pd2_ref[0:SUBLANES, 0:CONV_WIDTH] = pd2_ref[tile:tile + SUBLANES, 0:CONV_WIDTH]

    lane = lax.broadcasted_iota(jnp.int32, (tile, LANES), 1)
    low = lane < CONV_GROUP_DIM
    for blk in range(CONV_WIDTH // LANES):
        sl = slice(blk * LANES, (blk + 1) * LANES)
        yb = y[:, sl]
        y2 = yb * yb
        ss_lo = jnp.sum(jnp.where(low, y2, 0.0), axis=-1, keepdims=True)
        ss_hi = jnp.sum(jnp.where(low, 0.0, y2), axis=-1, keepdims=True)
        inv = lax.rsqrt(jnp.where(low, ss_lo, ss_hi) * (1.0 / CONV_GROUP_DIM) + NORM_EPS)
        mix_ref[:, sl] = (yb * inv * cng_ref[:, sl]).astype(BF16)

    outs = [jnp.dot(lhs_ref[n], rhs_ref[n], preferred_element_type=F32) for n in range(len(units))]

    for hd in range(RET_HEADS):
        hs = slice(hd * HEAD_DIM, (hd + 1) * HEAD_DIM)
        o = jnp.concatenate([outs[c * RET_HEADS + hd] for c in range(n_chunks)], axis=0)
        gate = zg[:, hs]
        gate = gate * (1.0 / (1.0 + jnp.exp(-gate)))
        yr = o * _rms_scale(o) * rng_ref[:, hs] * gate
        mix_ref[:, CONV_WIDTH + hd * HEAD_DIM:CONV_WIDTH + (hd + 1) * HEAD_DIM] = yr.astype(BF16)

    for rb in row_blocks:
        h_ref[rb, :] = x_ref[rb, :] + jnp.dot(mix_ref[rb, 0:D_MODEL], wout_ref[...], preferred_element_type=F32)


def _mlp_kernel(h_ref, g2_ref, wup_ref, wdn_ref, gf_ref, o_ref, hid_ref):
    h = h_ref[...]
    u = (h * _rms_scale(h) * g2_ref[...]).astype(BF16)
    for c in range(D_FF // FF_CHUNK):
        cols = slice(c * FF_CHUNK, (c + 1) * FF_CHUNK)
        a = jnp.maximum(_dot(u, wup_ref[:, cols]), 0.0)
        hid_ref[:, cols] = (a * a).astype(BF16)
    for r in range(h_ref.shape[0] // MLP_DOWN_ROWS):
        rows = slice(r * MLP_DOWN_ROWS, (r + 1) * MLP_DOWN_ROWS)
        y = h_ref[rows, :] + _dot(hid_ref[rows, :], wdn_ref[...])
        o_ref[rows, :] = y * _rms_scale(y) * gf_ref[...]


def _resident(shape):
    nd = len(shape)
    return pl.BlockSpec(shape, lambda *_: (0,) * nd, pipeline_mode=pl.Buffered(1))


@functools.lru_cache(maxsize=None)
def _retention_tables(seq):
    half = HEAD_DIM // 2
    inv_freq = 1.0 / (ROPE_BASE ** (np.arange(half, dtype=np.float64) / half))
    ang = np.arange(seq, dtype=np.float64)[:, None] * inv_freq[None, :]
    cos = np.cos(ang)
    sin = np.sin(ang)
    cos_t = np.concatenate([cos, cos], axis=-1)
    sin_t = np.concatenate([-sin, sin], axis=-1)

    log_gamma = np.log(1.0 - 2.0 ** (-5.0 - np.arange(RET_HEADS, dtype=np.float64)))
    idx = np.arange(CHUNK, dtype=np.float64)
    diff = idx[:, None] - idx[None, :]
    intra = np.where(diff[None] >= 0, np.exp(log_gamma[:, None, None] * np.maximum(diff, 0.0)[None]), 0.0)
    zeta = np.exp(log_gamma[:, None] * (CHUNK - 1 - idx)[None])
    xi = np.exp(log_gamma[:, None] * (idx + 1.0)[None])
    chunk_decay = np.exp(log_gamma * CHUNK)
    k_scale = HEAD_DIM ** -0.5
    decay_t = intra * k_scale
    zeta_t = np.broadcast_to((zeta * k_scale)[:, None, :], (RET_HEADS, HEAD_DIM, CHUNK))
    xi_t = np.broadcast_to(xi[:, :, None], (RET_HEADS, CHUNK, HEAD_DIM))
    return tuple(np.ascontiguousarray(t, dtype=np.float32)
                 for t in (cos_t, sin_t, decay_t, zeta_t, xi_t, chunk_decay))


def kernel(x, norm1_g, w_in, conv_w, conv_norm_g, ret_norm_g, w_out, norm2_g, w_up, w_down, final_norm_g):
    batch, seq, d_model = x.shape
    assert d_model == D_MODEL and w_in.shape == (D_MODEL, IN_COLS)
    assert seq % MIXER_TILE == 0 and MIXER_TILE % CHUNK == 0 and (batch * seq) % MLP_TILE == 0

    cos_t, sin_t, decay_t, zeta_t, xi_t, chunk_decay = _retention_tables(seq)
    row = lambda g: g.reshape(1, -1).astype(F32)
    n_units = (MIXER_TILE // CHUNK) * RET_HEADS

    tile_spec = pl.BlockSpec((None, MIXER_TILE, D_MODEL), lambda b, j: (b, j, 0))
    rope_spec = pl.BlockSpec((MIXER_TILE, HEAD_DIM), lambda b, j: (j, 0))
    seq_tiles = seq // MIXER_TILE
    n_steps = batch * seq_tiles
    assert D_MODEL % n_steps == 0 and D_FF % n_steps == 0
    wup_slab = pl.BlockSpec((D_MODEL // n_steps, D_FF), lambda b, j: (b * seq_tiles + j, 0))
    wdn_slab = pl.BlockSpec((D_FF // n_steps, D_MODEL), lambda b, j: (b * seq_tiles + j, 0))
    h, w_up_bf, w_down_bf = pl.pallas_call(
        _mixer_kernel,
        grid=(batch, seq_tiles),
        in_specs=[
            pl.BlockSpec(memory_space=pltpu.SMEM),
            tile_spec,
            _resident((1, D_MODEL)),
            pl.BlockSpec(memory_space=pl.ANY),
            _resident((CONV_K, CONV_WIDTH)),
            _resident((1, CONV_WIDTH)),
            _resident((1, RET_WIDTH)),
            pl.BlockSpec(memory_space=pl.ANY),
            rope_spec,
            rope_spec,
            _resident((RET_HEADS, CHUNK, CHUNK)),
            _resident((RET_HEADS, CHUNK, HEAD_DIM)),
            _resident((RET_HEADS, CHUNK, HEAD_DIM)),
            wup_slab,
            wdn_slab,
        ],
        out_specs=[tile_spec, wup_slab, wdn_slab],
        out_shape=[jax.ShapeDtypeStruct(x.shape, F32),
                   jax.ShapeDtypeStruct(w_up.shape, BF16),
                   jax.ShapeDtypeStruct(w_down.shape, BF16)],
        scratch_shapes=[
            pltpu.VMEM((D_MODEL, IN_COLS), BF16),
            pltpu.VMEM((D_MODEL, D_MODEL), BF16),
            pltpu.VMEM((RET_WIDTH, D_MODEL), BF16),
            pltpu.VMEM((CAST_SLOTS, CAST_ROWS, IN_COLS), F32),
            pltpu.VMEM((CAST_SLOTS, CAST_ROWS, D_MODEL), F32),
            pltpu.SemaphoreType.DMA((CAST_SLOTS,)),
            pltpu.VMEM((RET_HEADS, HEAD_DIM, HEAD_DIM), F32),
            pltpu.VMEM((MIXER_TILE + SUBLANES, CONV_WIDTH), F32),
            pltpu.VMEM((MIXER_TILE + SUBLANES, CONV_WIDTH), F32),
            pltpu.VMEM((MIXER_TILE, D_MODEL), BF16),
            pltpu.VMEM((n_units, CHUNK, 2 * CHUNK), BF16),
            pltpu.VMEM((n_units, 2 * CHUNK, HEAD_DIM), BF16),
        ],
        compiler_params=pltpu.CompilerParams(
            dimension_semantics=("arbitrary", "arbitrary"),
            vmem_limit_bytes=VMEM_LIMIT_BYTES),
        name="mixer",
    )(chunk_decay, x, row(norm1_g), w_in, conv_w, row(conv_norm_g), row(ret_norm_g),
      w_out, cos_t, sin_t, decay_t, zeta_t, xi_t, w_up, w_down)

    tokens = batch * seq
    tok_spec = pl.BlockSpec((MLP_TILE, D_MODEL), lambda i: (i, 0))
    out = pl.pallas_call(
        _mlp_kernel,
        grid=(tokens // MLP_TILE,),
        in_specs=[
            tok_spec,
            _resident((1, D_MODEL)),
            _resident((D_MODEL, D_FF)),
            _resident((D_FF, D_MODEL)),
            _resident((1, D_MODEL)),
        ],
        out_specs=tok_spec,
        out_shape=jax.ShapeDtypeStruct((tokens, D_MODEL), F32),
        scratch_shapes=[pltpu.VMEM((MLP_TILE, D_FF), BF16)],
        compiler_params=pltpu.CompilerParams(
            dimension_semantics=("arbitrary",),
            vmem_limit_bytes=VMEM_LIMIT_BYTES),
        name="mlp",
    )(h.reshape(tokens, D_MODEL), row(norm2_g), w_up_bf, w_down_bf, row(final_norm_g))
    return out.reshape(batch, seq, D_MODEL)
```

```python
import functools

import jax
import jax.numpy as jnp
import numpy as np
from jax import lax
from jax.experimental import pallas as pl
from jax.experimental.pallas import tpu as pltpu

D_MODEL = 1024
CONV_WIDTH = 512
CONV_GROUPS = 8
CONV_GROUP_DIM = CONV_WIDTH // CONV_GROUPS
CONV_K = 3
RET_WIDTH = 512
RET_HEADS = 4
HEAD_DIM = RET_WIDTH // RET_HEADS
CHUNK = 128
ROPE_BASE = 10000.0
D_FF = 4 * D_MODEL
NORM_EPS = 1e-6
IN_COLS = 3 * CONV_WIDTH + 4 * RET_WIDTH

LANES = 128
SUBLANES = 8
VMEM_LIMIT_BYTES = 56 * 1024 * 1024

MIXER_TILE = 1024
MLP_TILE = 1024
MLP_DOWN_ROWS = 256
FF_CHUNK = 1024
CAST_ROWS = 32
CAST_SLOTS = 8

BF16 = jnp.bfloat16
F32 = jnp.float32


def _dot(a, b):
    return jnp.dot(a, b, preferred_element_type=F32)


def _rms_scale(x):
    return lax.rsqrt(jnp.mean(x * x, axis=-1, keepdims=True) + NORM_EPS)


def _cast_rows_to_bf16(src_hbm, dst_ref, stage_ref, sem_ref):
    slots, rows = stage_ref.shape[0], stage_ref.shape[1]
    n = src_hbm.shape[0] // rows

    def copy(k):
        slot = k % slots
        return pltpu.make_async_copy(src_hbm.at[pl.ds(k * rows, rows), :], stage_ref.at[slot], sem_ref.at[slot])

    for k in range(min(slots, n)):
        copy(k).start()
    for k in range(n):
        copy(k).wait()
        dst_ref[k * rows:(k + 1) * rows, :] = stage_ref[k % slots].astype(BF16)
        if k + slots < n:
            copy(k + slots).start()


def _mixer_kernel(cd_ref, x_ref, g1_ref, win_hbm, convw_ref, cng_ref, rng_ref, wout_hbm,
                  cos_ref, sin_ref, decay_ref, zeta_ref, xi_ref, wup_ref, wdn_ref,
                  h_ref, wup_bf_ref, wdn_bf_ref,
                  win_ref, wout_ref, win_stage_ref, wout_stage_ref, cast_sem_ref,
                  state_ref, pd1_ref, pd2_ref, mix_ref, lhs_ref, rhs_ref):
    tile = x_ref.shape[0]
    n_chunks = tile // CHUNK
    units = [(c, hd) for c in range(n_chunks) for hd in range(RET_HEADS)]

    @pl.when((pl.program_id(0) == 0) & (pl.program_id(1) == 0))
    def _():
        _cast_rows_to_bf16(win_hbm, win_ref, win_stage_ref, cast_sem_ref)
        _cast_rows_to_bf16(wout_hbm, wout_ref, wout_stage_ref, cast_sem_ref)

    wup_bf_ref[...] = wup_ref[...].astype(BF16)
    wdn_bf_ref[...] = wdn_ref[...].astype(BF16)

    @pl.when(pl.program_id(1) == 0)
    def _():
        state_ref[...] = jnp.zeros_like(state_ref)
        zeros = jnp.zeros((SUBLANES, CONV_WIDTH), F32)
        pd1_ref[0:SUBLANES, :] = zeros
        pd2_ref[0:SUBLANES, :] = zeros
        pd1_ref[tile:tile + SUBLANES, :] = zeros
        pd2_ref[tile:tile + SUBLANES, :] = zeros

    x = x_ref[...]
    u = (x * _rms_scale(x) * g1_ref[...]).astype(BF16)

    def in_proj(col0, width):
        return jnp.dot(u, win_ref[:, col0:col0 + width], preferred_element_type=F32)

    c0 = 3 * CONV_WIDTH

    zq = in_proj(c0, RET_WIDTH)
    zk = in_proj(c0 + RET_WIDTH, RET_WIDTH)
    cb = in_proj(0, CONV_WIDTH)
    cc = in_proj(CONV_WIDTH, CONV_WIDTH)

    cos = cos_ref[...]
    sin = sin_ref[...]
    q_rot, kt_rot = [], []
    for hd in range(RET_HEADS):
        hs = slice(hd * HEAD_DIM, (hd + 1) * HEAD_DIM)
        qh = zq[:, hs]
        kh = zk[:, hs]
        q_rot.append(qh * cos + pltpu.roll(qh, HEAD_DIM // 2, axis=1) * sin)
        kt_rot.append((kh * cos + pltpu.roll(kh, HEAD_DIM // 2, axis=1) * sin).T)

    for n, (c, hd) in enumerate(units):
        rows = slice(c * CHUNK, (c + 1) * CHUNK)
        q = q_rot[hd][rows]
        kt = kt_rot[hd][:, rows]
        s = jnp.dot(q.astype(BF16), kt.astype(BF16), preferred_element_type=F32) * decay_ref[hd]
        lhs_ref[n, :, 0:CHUNK] = s.astype(BF16)
        lhs_ref[n, :, CHUNK:2 * CHUNK] = (q * xi_ref[hd]).astype(BF16)

    ch = in_proj(2 * CONV_WIDTH, CONV_WIDTH)
    zv = in_proj(c0 + 2 * RET_WIDTH, RET_WIDTH)
    v_bf = zv.astype(BF16)

    states = [state_ref[hd] for hd in range(RET_HEADS)]
    for n, (c, hd) in enumerate(units):
        rows = slice(c * CHUNK, (c + 1) * CHUNK)
        v = v_bf[rows, hd * HEAD_DIM:(hd + 1) * HEAD_DIM]
        kzt = (kt_rot[hd][:, rows] * zeta_ref[hd]).astype(BF16)
        kv = jnp.dot(kzt, v, preferred_element_type=F32)
        rhs_ref[n, 0:CHUNK, :] = v
        rhs_ref[n, CHUNK:2 * CHUNK, :] = states[hd].astype(BF16)
        states[hd] = cd_ref[hd] * states[hd] + kv
    for hd in range(RET_HEADS):
        state_ref[hd] = states[hd]

    zg = in_proj(c0 + 3 * RET_WIDTH, RET_WIDTH)

    p = cc * ch
    pd1_ref[1:1 + tile, :] = p
    pd2_ref[2:2 + tile, :] = p
    p1 = pd1_ref[0:tile, :]
    p2 = pd2_ref[0:tile, :]
    y = cb * (p2 * convw_ref[0:1, :] + p1 * convw_ref[1:2, :] + p * convw_ref[2:3, :])
    pd1_ref[0:SUBLANES, :] = pd1_ref[tile:tile + SUBLANES, :]
    pd2_ref[0:SUBLANES, :] = pd2_ref[tile:tile + SUBLANES, :]

    lane = lax.broadcasted_iota(jnp.int32, (tile, LANES), 1)
    low = lane < CONV_GROUP_DIM
    for blk in range(CONV_WIDTH // LANES):
        sl = slice(blk * LANES, (blk + 1) * LANES)
        yb = y[:, sl]
        y2 = yb * yb
        ss_lo = jnp.sum(jnp.where(low, y2, 0.0), axis=-1, keepdims=True)
        ss_hi = jnp.sum(jnp.where(low, 0.0, y2), axis=-1, keepdims=True)
        inv = lax.rsqrt(jnp.where(low, ss_lo, ss_hi) * (1.0 / CONV_GROUP_DIM) + NORM_EPS)
        mix_ref[:, sl] = (yb * inv * cng_ref[:, sl]).astype(BF16)

    outs = [jnp.dot(lhs_ref[n], rhs_ref[n], preferred_element_type=F32) for n in range(len(units))]
    for hd in range(RET_HEADS):
        hs = slice(hd * HEAD_DIM, (hd + 1) * HEAD_DIM)
        o = jnp.concatenate([outs[c * RET_HEADS + hd] for c in range(n_chunks)], axis=0)
        gate = zg[:, hs]
        gate = gate * (1.0 / (1.0 + jnp.exp(-gate)))
        yr = o * _rms_scale(o) * rng_ref[:, hs] * gate
        mix_ref[:, CONV_WIDTH + hd * HEAD_DIM:CONV_WIDTH + (hd + 1) * HEAD_DIM] = yr.astype(BF16)

    h_ref[...] = x_ref[...] + jnp.dot(mix_ref[...], wout_ref[...], preferred_element_type=F32)


def _mlp_kernel(h_ref, g2_ref, wup_ref, wdn_ref, gf_ref, o_ref, hid_ref):
    h = h_ref[...]
    u = (h * _rms_scale(h) * g2_ref[...]).astype(BF16)
    for c in range(D_FF // FF_CHUNK):
        cols = slice(c * FF_CHUNK, (c + 1) * FF_CHUNK)
        a = jnp.maximum(_dot(u, wup_ref[:, cols]), 0.0)
        hid_ref[:, cols] = (a * a).astype(BF16)
    for r in range(h_ref.shape[0] // MLP_DOWN_ROWS):
        rows = slice(r * MLP_DOWN_ROWS, (r + 1) * MLP_DOWN_ROWS)
        y = h_ref[rows, :] + _dot(hid_ref[rows, :], wdn_ref[...])
        o_ref[rows, :] = y * _rms_scale(y) * gf_ref[...]


def _resident(shape):
    nd = len(shape)
    return pl.BlockSpec(shape, lambda *_: (0,) * nd, pipeline_mode=pl.Buffered(1))


@functools.lru_cache(maxsize=None)
def _retention_tables(seq):
    half = HEAD_DIM // 2
    inv_freq = 1.0 / (ROPE_BASE ** (np.arange(half, dtype=np.float64) / half))
    ang = np.arange(seq, dtype=np.float64)[:, None] * inv_freq[None, :]
    cos = np.cos(ang)
    sin = np.sin(ang)
    cos_t = np.concatenate([cos, cos], axis=-1)
    sin_t = np.concatenate([-sin, sin], axis=-1)

    log_gamma = np.log(1.0 - 2.0 ** (-5.0 - np.arange(RET_HEADS, dtype=np.float64)))
    idx = np.arange(CHUNK, dtype=np.float64)
    diff = idx[:, None] - idx[None, :]
    intra = np.where(diff[None] >= 0, np.exp(log_gamma[:, None, None] * np.maximum(diff, 0.0)[None]), 0.0)
    zeta = np.exp(log_gamma[:, None] * (CHUNK - 1 - idx)[None])
    xi = np.exp(log_gamma[:, None] * (idx + 1.0)[None])
    chunk_decay = np.exp(log_gamma * CHUNK)
    k_scale = HEAD_DIM ** -0.5
    decay_t = intra * k_scale
    zeta_t = np.broadcast_to((zeta * k_scale)[:, None, :], (RET_HEADS, HEAD_DIM, CHUNK))
    xi_t = np.broadcast_to(xi[:, :, None], (RET_HEADS, CHUNK, HEAD_DIM))
    return tuple(np.ascontiguousarray(t, dtype=np.float32)
                 for t in (cos_t, sin_t, decay_t, zeta_t, xi_t, chunk_decay))


def kernel(x, norm1_g, w_in, conv_w, conv_norm_g, ret_norm_g, w_out, norm2_g, w_up, w_down, final_norm_g):
    batch, seq, d_model = x.shape
    assert d_model == D_MODEL and w_in.shape == (D_MODEL, IN_COLS)
    assert seq % MIXER_TILE == 0 and MIXER_TILE % CHUNK == 0 and (batch * seq) % MLP_TILE == 0

    cos_t, sin_t, decay_t, zeta_t, xi_t, chunk_decay = _retention_tables(seq)
    row = lambda g: g.reshape(1, -1).astype(F32)
    n_units = (MIXER_TILE // CHUNK) * RET_HEADS

    tile_spec = pl.BlockSpec((None, MIXER_TILE, D_MODEL), lambda b, j: (b, j, 0))
    rope_spec = pl.BlockSpec((MIXER_TILE, HEAD_DIM), lambda b, j: (j, 0))
    seq_tiles = seq // MIXER_TILE
    n_steps = batch * seq_tiles
    assert D_MODEL % n_steps == 0 and D_FF % n_steps == 0
    wup_slab = pl.BlockSpec((D_MODEL // n_steps, D_FF), lambda b, j: (b * seq_tiles + j, 0))
    wdn_slab = pl.BlockSpec((D_FF // n_steps, D_MODEL), lambda b, j: (b * seq_tiles + j, 0))
    h, w_up_bf, w_down_bf = pl.pallas_call(
        _mixer_kernel,
        grid=(batch, seq_tiles),
        in_specs=[
            pl.BlockSpec(memory_space=pltpu.SMEM),
            tile_spec,
            _resident((1, D_MODEL)),
            pl.BlockSpec(memory_space=pl.ANY),
            _resident((CONV_K, CONV_WIDTH)),
            _resident((1, CONV_WIDTH)),
            _resident((1, RET_WIDTH)),
            pl.BlockSpec(memory_space=pl.ANY),
            rope_spec,
            rope_spec,
            _resident((RET_HEADS, CHUNK, CHUNK)),
            _resident((RET_HEADS, CHUNK, HEAD_DIM)),
            _resident((RET_HEADS, CHUNK, HEAD_DIM)),
            wup_slab,
            wdn_slab,
        ],
        out_specs=[tile_spec, wup_slab, wdn_slab],
        out_shape=[jax.ShapeDtypeStruct(x.shape, F32),
                   jax.ShapeDtypeStruct(w_up.shape, BF16),
                   jax.ShapeDtypeStruct(w_down.shape, BF16)],
        scratch_shapes=[
            pltpu.VMEM((D_MODEL, IN_COLS), BF16),
            pltpu.VMEM((D_MODEL, D_MODEL), BF16),
            pltpu.VMEM((CAST_SLOTS, CAST_ROWS, IN_COLS), F32),
            pltpu.VMEM((CAST_SLOTS, CAST_ROWS, D_MODEL), F32),
            pltpu.SemaphoreType.DMA((CAST_SLOTS,)),
            pltpu.VMEM((RET_HEADS, HEAD_DIM, HEAD_DIM), F32),
            pltpu.VMEM((MIXER_TILE + SUBLANES, CONV_WIDTH), F32),
            pltpu.VMEM((MIXER_TILE + SUBLANES, CONV_WIDTH), F32),
            pltpu.VMEM((MIXER_TILE, D_MODEL), BF16),
            pltpu.VMEM((n_units, CHUNK, 2 * CHUNK), BF16),
            pltpu.VMEM((n_units, 2 * CHUNK, HEAD_DIM), BF16),
        ],
        compiler_params=pltpu.CompilerParams(
            dimension_semantics=("arbitrary", "arbitrary"),
            vmem_limit_bytes=VMEM_LIMIT_BYTES),
        name="mixer",
    )(chunk_decay, x, row(norm1_g), w_in, conv_w, row(conv_norm_g), row(ret_norm_g),
      w_out, cos_t, sin_t, decay_t, zeta_t, xi_t, w_up, w_down)

    tokens = batch * seq
    tok_spec = pl.BlockSpec((MLP_TILE, D_MODEL), lambda i: (i, 0))
    out = pl.pallas_call(
        _mlp_kernel,
        grid=(tokens // MLP_TILE,),
        in_specs=[
            tok_spec,
            _resident((1, D_MODEL)),
            _resident((D_MODEL, D_FF)),
            _resident((D_FF, D_MODEL)),
            _resident((1, D_MODEL)),
        ],
        out_specs=tok_spec,
        out_shape=jax.ShapeDtypeStruct((tokens, D_MODEL), F32),
        scratch_shapes=[pltpu.VMEM((MLP_TILE, D_FF), BF16)],
        compiler_params=pltpu.CompilerParams(
            dimension_semantics=("arbitrary",),
            vmem_limit_bytes=VMEM_LIMIT_BYTES),
        name="mlp",
    )(h.reshape(tokens, D_MODEL), row(norm2_g), w_up_bf, w_down_bf, row(final_norm_g))
    return out.reshape(batch, seq, D_MODEL)
```

```python
import functools

import jax
import jax.numpy as jnp
import numpy as np
from jax import lax
from jax.experimental import pallas as pl
from jax.experimental.pallas import tpu as pltpu

D_MODEL = 1024
CONV_WIDTH = 512
CONV_GROUPS = 8
CONV_GROUP_DIM = CONV_WIDTH // CONV_GROUPS
CONV_K = 3
RET_WIDTH = 512
RET_HEADS = 4
HEAD_DIM = RET_WIDTH // RET_HEADS
CHUNK = 128
ROPE_BASE = 10000.0
D_FF = 4 * D_MODEL
NORM_EPS = 1e-6
IN_COLS = 3 * CONV_WIDTH + 4 * RET_WIDTH

LANES = 128
SUBLANES = 8
VMEM_LIMIT_BYTES = 56 * 1024 * 1024

MIXER_TILE = 1024
MLP_TILE = 1024
MLP_DOWN_ROWS = 512
FF_CHUNK = 1024
CAST_ROWS = 32
CAST_SLOTS = 8

BF16 = jnp.bfloat16
F32 = jnp.float32


def _dot(a, b):
    return jnp.dot(a, b, preferred_element_type=F32)


def _rms_scale(x):
    return lax.rsqrt(jnp.mean(x * x, axis=-1, keepdims=True) + NORM_EPS)


def _bf16_cast_ring(src_hbm, dst_ref, stage_ref, sem_ref):
    slots, rows = stage_ref.shape[0], stage_ref.shape[1]
    n = src_hbm.shape[0] // rows

    def copy(k):
        slot = k % slots
        return pltpu.make_async_copy(src_hbm.at[pl.ds(k * rows, rows), :], stage_ref.at[slot], sem_ref.at[slot])

    def prime():
        for k in range(min(slots, n)):
            copy(k).start()

    def drain():
        for k in range(n):
            copy(k).wait()
            dst_ref[k * rows:(k + 1) * rows, :] = stage_ref[k % slots].astype(BF16)
            if k + slots < n:
                copy(k + slots).start()

    return prime, drain


def _mixer_kernel(cd_ref, x_ref, g1_ref, win_hbm, convw_ref, cng_ref, rng_ref, wout_hbm,
                  cos_ref, sin_ref, decay_ref, zeta_ref, xi_ref, wup_ref, wdn_ref,
                  h_ref, wup_bf_ref, wdn_bf_ref,
                  win_ref, wout_ref, win_stage_ref, wout_stage_ref, win_sem_ref, wout_sem_ref,
                  state_ref, pd1_ref, pd2_ref, mix_ref, lhs_ref, rhs_ref):
    tile = x_ref.shape[0]
    n_chunks = tile // CHUNK
    units = [(c, hd) for hd in range(RET_HEADS) for c in range(n_chunks)]
    unit_index = {unit: n for n, unit in enumerate(units)}

    @pl.when((pl.program_id(0) == 0) & (pl.program_id(1) == 0))
    def _():
        prime_in, drain_in = _bf16_cast_ring(win_hbm, win_ref, win_stage_ref, win_sem_ref)
        prime_out, drain_out = _bf16_cast_ring(wout_hbm, wout_ref, wout_stage_ref, wout_sem_ref)
        prime_in()
        prime_out()
        drain_in()
        drain_out()

    wup_bf_ref[...] = wup_ref[...].astype(BF16)
    wdn_bf_ref[...] = wdn_ref[...].astype(BF16)

    @pl.when(pl.program_id(1) == 0)
    def _():
        state_ref[...] = jnp.zeros_like(state_ref)
        zeros = jnp.zeros((SUBLANES, CONV_WIDTH), F32)
        pd1_ref[0:SUBLANES, :] = zeros
        pd2_ref[0:SUBLANES, :] = zeros
        pd1_ref[tile:tile + SUBLANES, :] = zeros
        pd2_ref[tile:tile + SUBLANES, :] = zeros

    x = x_ref[...]
    u = (x * _rms_scale(x) * g1_ref[...]).astype(BF16)

    def in_proj(col0, width):
        return jnp.dot(u, win_ref[:, col0:col0 + width], preferred_element_type=F32)

    c0 = 3 * CONV_WIDTH

    zq = in_proj(c0, RET_WIDTH)
    zk = in_proj(c0 + RET_WIDTH, RET_WIDTH)
    cb = in_proj(0, CONV_WIDTH)
    cc = in_proj(CONV_WIDTH, CONV_WIDTH)

    cos = cos_ref[...]
    sin = sin_ref[...]
    q_rot, kt_rot = [], []
    for hd in range(RET_HEADS):
        hs = slice(hd * HEAD_DIM, (hd + 1) * HEAD_DIM)
        qh = zq[:, hs]
        kh = zk[:, hs]
        q_rot.append(qh * cos + pltpu.roll(qh, HEAD_DIM // 2, axis=1) * sin)
        kt_rot.append((kh * cos + pltpu.roll(kh, HEAD_DIM // 2, axis=1) * sin).T)

    for n, (c, hd) in enumerate(units):
        rows = slice(c * CHUNK, (c + 1) * CHUNK)
        q = q_rot[hd][rows]
        kt = kt_rot[hd][:, rows]
        s = jnp.dot(q.astype(BF16), kt.astype(BF16), preferred_element_type=F32) * decay_ref[hd]
        lhs_ref[n, :, 0:CHUNK] = s.astype(BF16)
        lhs_ref[n, :, CHUNK:2 * CHUNK] = (q * xi_ref[hd]).astype(BF16)

    ch = in_proj(2 * CONV_WIDTH, CONV_WIDTH)
    zv = in_proj(c0 + 2 * RET_WIDTH, RET_WIDTH)
    v_bf = zv.astype(BF16)

    states = [state_ref[hd] for hd in range(RET_HEADS)]
    for n, (c, hd) in enumerate(units):
        rows = slice(c * CHUNK, (c + 1) * CHUNK)
        v = v_bf[rows, hd * HEAD_DIM:(hd + 1) * HEAD_DIM]
        kzt = (kt_rot[hd][:, rows] * zeta_ref[hd]).astype(BF16)
        kv = jnp.dot(kzt, v, preferred_element_type=F32)
        rhs_ref[n, 0:CHUNK, :] = v
        rhs_ref[n, CHUNK:2 * CHUNK, :] = states[hd].astype(BF16)
        states[hd] = cd_ref[hd] * states[hd] + kv
    for hd in range(RET_HEADS):
        state_ref[hd] = states[hd]

    zg = in_proj(c0 + 3 * RET_WIDTH, RET_WIDTH)

    p = cc * ch
    pd1_ref[1:1 + tile, :] = p
    pd2_ref[2:2 + tile, :] = p
    p1 = pd1_ref[0:tile, :]
    p2 = pd2_ref[0:tile, :]
    y = cb * (p2 * convw_ref[0:1, :] + p1 * convw_ref[1:2, :] + p * convw_ref[2:3, :])
    pd1_ref[0:SUBLANES, :] = pd1_ref[tile:tile + SUBLANES, :]
    pd2_ref[0:SUBLANES, :] = pd2_ref[tile:tile + SUBLANES, :]

    lane = lax.broadcasted_iota(jnp.int32, (tile, LANES), 1)
    low = lane < CONV_GROUP_DIM
    for blk in range(CONV_WIDTH // LANES):
        sl = slice(blk * LANES, (blk + 1) * LANES)
        yb = y[:, sl]
        y2 = yb * yb
        ss_lo = jnp.sum(jnp.where(low, y2, 0.0), axis=-1, keepdims=True)
        ss_hi = jnp.sum(jnp.where(low, 0.0, y2), axis=-1, keepdims=True)
        inv = lax.rsqrt(jnp.where(low, ss_lo, ss_hi) * (1.0 / CONV_GROUP_DIM) + NORM_EPS)
        mix_ref[:, sl] = (yb * inv * cng_ref[:, sl]).astype(BF16)

    outs = [jnp.dot(lhs_ref[n], rhs_ref[n], preferred_element_type=F32) for n in range(len(units))]
    for hd in range(RET_HEADS):
        hs = slice(hd * HEAD_DIM, (hd + 1) * HEAD_DIM)
        o = jnp.concatenate([outs[unit_index[(c, hd)]] for c in range(n_chunks)], axis=0)
        gate = zg[:, hs]
        gate = gate * (1.0 / (1.0 + jnp.exp(-gate)))
        yr = o * _rms_scale(o) * rng_ref[:, hs] * gate
        mix_ref[:, CONV_WIDTH + hd * HEAD_DIM:CONV_WIDTH + (hd + 1) * HEAD_DIM] = yr.astype(BF16)

    h_ref[...] = x_ref[...] + jnp.dot(mix_ref[...], wout_ref[...], preferred_element_type=F32)


def _mlp_kernel(h_ref, g2_ref, wup_ref, wdn_ref, gf_ref, o_ref, hid_ref):
    h = h_ref[...]
    u = (h * _rms_scale(h) * g2_ref[...]).astype(BF16)
    for c in range(D_FF // FF_CHUNK):
        cols = slice(c * FF_CHUNK, (c + 1) * FF_CHUNK)
        a = jnp.maximum(_dot(u, wup_ref[:, cols]), 0.0)
        hid_ref[:, cols] = (a * a).astype(BF16)
    for r in range(h_ref.shape[0] // MLP_DOWN_ROWS):
        rows = slice(r * MLP_DOWN_ROWS, (r + 1) * MLP_DOWN_ROWS)
        y = h_ref[rows, :] + _dot(hid_ref[rows, :], wdn_ref[...])
        o_ref[rows, :] = y * _rms_scale(y) * gf_ref[...]


def _resident(shape):
    nd = len(shape)
    return pl.BlockSpec(shape, lambda *_: (0,) * nd, pipeline_mode=pl.Buffered(1))


@functools.lru_cache(maxsize=None)
def _retention_tables(seq):
    half = HEAD_DIM // 2
    inv_freq = 1.0 / (ROPE_BASE ** (np.arange(half, dtype=np.float64) / half))
    ang = np.arange(seq, dtype=np.float64)[:, None] * inv_freq[None, :]
    cos = np.cos(ang)
    sin = np.sin(ang)
    cos_t = np.concatenate([cos, cos], axis=-1)
    sin_t = np.concatenate([-sin, sin], axis=-1)

    log_gamma = np.log(1.0 - 2.0 ** (-5.0 - np.arange(RET_HEADS, dtype=np.float64)))
    idx = np.arange(CHUNK, dtype=np.float64)
    diff = idx[:, None] - idx[None, :]
    intra = np.where(diff[None] >= 0, np.exp(log_gamma[:, None, None] * np.maximum(diff, 0.0)[None]), 0.0)
    zeta = np.exp(log_gamma[:, None] * (CHUNK - 1 - idx)[None])
    xi = np.exp(log_gamma[:, None] * (idx + 1.0)[None])
    chunk_decay = np.exp(log_gamma * CHUNK)
    k_scale = HEAD_DIM ** -0.5
    decay_t = intra * k_scale
    zeta_t = np.broadcast_to((zeta * k_scale)[:, None, :], (RET_HEADS, HEAD_DIM, CHUNK))
    xi_t = np.broadcast_to(xi[:, :, None], (RET_HEADS, CHUNK, HEAD_DIM))
    return tuple(np.ascontiguousarray(t, dtype=np.float32)
                 for t in (cos_t, sin_t, decay_t, zeta_t, xi_t, chunk_decay))


def kernel(x, norm1_g, w_in, conv_w, conv_norm_g, ret_norm_g, w_out, norm2_g, w_up, w_down, final_norm_g):
    batch, seq, d_model = x.shape
    assert d_model == D_MODEL and w_in.shape == (D_MODEL, IN_COLS)
    assert seq % MIXER_TILE == 0 and MIXER_TILE % CHUNK == 0 and (batch * seq) % MLP_TILE == 0

    cos_t, sin_t, decay_t, zeta_t, xi_t, chunk_decay = _retention_tables(seq)
    row = lambda g: g.reshape(1, -1).astype(F32)
    n_units = (MIXER_TILE // CHUNK) * RET_HEADS

    tile_spec = pl.BlockSpec((None, MIXER_TILE, D_MODEL), lambda b, j: (b, j, 0))
    rope_spec = pl.BlockSpec((MIXER_TILE, HEAD_DIM), lambda b, j: (j, 0))
    seq_tiles = seq // MIXER_TILE
    n_steps = batch * seq_tiles
    assert D_MODEL % n_steps == 0 and D_FF % n_steps == 0
    wup_slab = pl.BlockSpec((D_MODEL // n_steps, D_FF), lambda b, j: (b * seq_tiles + j, 0))
    wdn_slab = pl.BlockSpec((D_FF // n_steps, D_MODEL), lambda b, j: (b * seq_tiles + j, 0))
    h, w_up_bf, w_down_bf = pl.pallas_call(
        _mixer_kernel,
        grid=(batch, seq_tiles),
        in_specs=[
            pl.BlockSpec(memory_space=pltpu.SMEM),
            tile_spec,
            _resident((1, D_MODEL)),
            pl.BlockSpec(memory_space=pl.ANY),
            _resident((CONV_K, CONV_WIDTH)),
            _resident((1, CONV_WIDTH)),
            _resident((1, RET_WIDTH)),
            pl.BlockSpec(memory_space=pl.ANY),
            rope_spec,
            rope_spec,
            _resident((RET_HEADS, CHUNK, CHUNK)),
            _resident((RET_HEADS, CHUNK, HEAD_DIM)),
            _resident((RET_HEADS, CHUNK, HEAD_DIM)),
            wup_slab,
            wdn_slab,
        ],
        out_specs=[tile_spec, wup_slab, wdn_slab],
        out_shape=[jax.ShapeDtypeStruct(x.shape, F32),
                   jax.ShapeDtypeStruct(w_up.shape, BF16),
                   jax.ShapeDtypeStruct(w_down.shape, BF16)],
        scratch_shapes=[
            pltpu.VMEM((D_MODEL, IN_COLS), BF16),
            pltpu.VMEM((D_MODEL, D_MODEL), BF16),
            pltpu.VMEM((CAST_SLOTS, CAST_ROWS, IN_COLS), F32),
            pltpu.VMEM((CAST_SLOTS, CAST_ROWS, D_MODEL), F32),
            pltpu.SemaphoreType.DMA((CAST_SLOTS,)),
            pltpu.SemaphoreType.DMA((CAST_SLOTS,)),
            pltpu.VMEM((RET_HEADS, HEAD_DIM, HEAD_DIM), F32),
            pltpu.VMEM((MIXER_TILE + SUBLANES, CONV_WIDTH), F32),
            pltpu.VMEM((MIXER_TILE + SUBLANES, CONV_WIDTH), F32),
            pltpu.VMEM((MIXER_TILE, D_MODEL), BF16),
            pltpu.VMEM((n_units, CHUNK, 2 * CHUNK), BF16),
            pltpu.VMEM((n_units, 2 * CHUNK, HEAD_DIM), BF16),
        ],
        compiler_params=pltpu.CompilerParams(
            dimension_semantics=("arbitrary", "arbitrary"),
            vmem_limit_bytes=VMEM_LIMIT_BYTES),
        name="mixer",
    )(chunk_decay, x, row(norm1_g), w_in, conv_w, row(conv_norm_g), row(ret_norm_g),
      w_out, cos_t, sin_t, decay_t, zeta_t, xi_t, w_up, w_down)

    tokens = batch * seq
    tok_spec = pl.BlockSpec((MLP_TILE, D_MODEL), lambda i: (i, 0))
    out = pl.pallas_call(
        _mlp_kernel,
        grid=(tokens // MLP_TILE,),
        in_specs=[
            tok_spec,
            _resident((1, D_MODEL)),
            _resident((D_MODEL, D_FF)),
            _resident((D_FF, D_MODEL)),
            _resident((1, D_MODEL)),
        ],
        out_specs=tok_spec,
        out_shape=jax.ShapeDtypeStruct((tokens, D_MODEL), F32),
        scratch_shapes=[pltpu.VMEM((MLP_TILE, D_FF), BF16)],
        compiler_params=pltpu.CompilerParams(
            dimension_semantics=("arbitrary",),
            vmem_limit_bytes=VMEM_LIMIT_BYTES),
        name="mlp",
    )(h.reshape(tokens, D_MODEL), row(norm2_g), w_up_bf, w_down_bf, row(final_norm_g))
    return out.reshape(batch, seq, D_MODEL)
```

```python
import functools

import jax
import jax.numpy as jnp
import numpy as np
from jax import lax
from jax.experimental import pallas as pl
from jax.experimental.pallas import tpu as pltpu

D_MODEL = 1024
CONV_WIDTH = 512
CONV_GROUPS = 8
CONV_GROUP_DIM = CONV_WIDTH // CONV_GROUPS
CONV_K = 3
RET_WIDTH = 512
RET_HEADS = 4
HEAD_DIM = RET_WIDTH // RET_HEADS
CHUNK = 128
ROPE_BASE = 10000.0
D_FF = 4 * D_MODEL
NORM_EPS = 1e-6
IN_COLS = 3 * CONV_WIDTH + 4 * RET_WIDTH

LANES = 128
SUBLANES = 8
VMEM_LIMIT_BYTES = 56 * 1024 * 1024

MIXER_TILE = 1024
MLP_TILE = 1024
MLP_DOWN_ROWS = 256
FF_CHUNK = 1024
WARM_K = 256
CAST_ROWS = 32
CAST_SLOTS = 8

BF16 = jnp.bfloat16
F32 = jnp.float32


def _dot(a, b):
    return jnp.dot(a, b, preferred_element_type=F32)


def _rms_scale(x):
    return lax.rsqrt(jnp.mean(x * x, axis=-1, keepdims=True) + NORM_EPS)


def _bf16_cast_ring(src_hbm, dst_ref, stage_ref, sem_ref):
    slots, rows = stage_ref.shape[0], stage_ref.shape[1]
    n = src_hbm.shape[0] // rows

    def copy(k):
        slot = k % slots
        return pltpu.make_async_copy(src_hbm.at[pl.ds(k * rows, rows), :], stage_ref.at[slot], sem_ref.at[slot])

    def prime():
        for k in range(min(slots, n)):
            copy(k).start()

    def drain():
        for k in range(n):
            copy(k).wait()
            dst_ref[k * rows:(k + 1) * rows, :] = stage_ref[k % slots].astype(BF16)
            if k + slots < n:
                copy(k + slots).start()

    return prime, drain


def _mixer_kernel(cd_ref, x_ref, g1_ref, win_hbm, convw_ref, cng_ref, rng_ref, wout_hbm,
                  cos_ref, sin_ref, decay_ref, zeta_ref, xi_ref, wup_ref, wdn_ref,
                  h_ref, wup_bf_ref, wdn_bf_ref,
                  win_ref, wout_ref, win_stage_ref, wout_stage_ref, win_sem_ref, wout_sem_ref,
                  state_ref, pd1_ref, pd2_ref, mix_ref, lhs_ref, rhs_ref):
    tile = x_ref.shape[0]
    n_chunks = tile // CHUNK
    units = [(c, hd) for c in range(n_chunks) for hd in range(RET_HEADS)]

    @pl.when((pl.program_id(0) == 0) & (pl.program_id(1) == 0))
    def _():
        prime_in, drain_in = _bf16_cast_ring(win_hbm, win_ref, win_stage_ref, win_sem_ref)
        prime_out, drain_out = _bf16_cast_ring(wout_hbm, wout_ref, wout_stage_ref, wout_sem_ref)
        prime_in()
        prime_out()
        drain_in()
        drain_out()

    wup_bf_ref[...] = wup_ref[...].astype(BF16)
    wdn_bf_ref[...] = wdn_ref[...].astype(BF16)

    @pl.when(pl.program_id(1) == 0)
    def _():
        state_ref[...] = jnp.zeros_like(state_ref)
        zeros = jnp.zeros((SUBLANES, CONV_WIDTH), F32)
        pd1_ref[0:SUBLANES, :] = zeros
        pd2_ref[0:SUBLANES, :] = zeros
        pd1_ref[tile:tile + SUBLANES, :] = zeros
        pd2_ref[tile:tile + SUBLANES, :] = zeros

    warm = jnp.dot(jnp.zeros((tile, WARM_K), BF16), win_ref[0:WARM_K, 0:2 * LANES], preferred_element_type=F32)

    x = x_ref[...]
    u = (x * _rms_scale(x) * g1_ref[...]).astype(BF16)

    def in_proj(col0, width):
        return jnp.dot(u, win_ref[:, col0:col0 + width], preferred_element_type=F32)

    c0 = 3 * CONV_WIDTH

    zq = in_proj(c0, RET_WIDTH)
    zk = in_proj(c0 + RET_WIDTH, RET_WIDTH)
    cb = in_proj(0, CONV_WIDTH)
    cc = in_proj(CONV_WIDTH, CONV_WIDTH)

    cos = cos_ref[...]
    sin = sin_ref[...]
    q_rot, kt_rot = [], []
    for hd in range(RET_HEADS):
        hs = slice(hd * HEAD_DIM, (hd + 1) * HEAD_DIM)
        qh = zq[:, hs]
        kh = zk[:, hs]
        q_rot.append(qh * cos + pltpu.roll(qh, HEAD_DIM // 2, axis=1) * sin)
        kt_rot.append((kh * cos + pltpu.roll(kh, HEAD_DIM // 2, axis=1) * sin).T)

    for n, (c, hd) in enumerate(units):
        rows = slice(c * CHUNK, (c + 1) * CHUNK)
        q = q_rot[hd][rows]
        kt = kt_rot[hd][:, rows]
        s = jnp.dot(q.astype(BF16), kt.astype(BF16), preferred_element_type=F32) * decay_ref[hd]
        lhs_ref[n, :, 0:CHUNK] = s.astype(BF16)
        lhs_ref[n, :, CHUNK:2 * CHUNK] = (q * xi_ref[hd]).astype(BF16)

    ch = in_proj(2 * CONV_WIDTH, CONV_WIDTH)
    zv = in_proj(c0 + 2 * RET_WIDTH, RET_WIDTH)
    v_bf = zv.astype(BF16)

    states = [state_ref[hd] for hd in range(RET_HEADS)]
    for n, (c, hd) in enumerate(units):
        rows = slice(c * CHUNK, (c + 1) * CHUNK)
        v = v_bf[rows, hd * HEAD_DIM:(hd + 1) * HEAD_DIM]
        kzt = (kt_rot[hd][:, rows] * zeta_ref[hd]).astype(BF16)
        kv = jnp.dot(kzt, v, preferred_element_type=F32)
        rhs_ref[n, 0:CHUNK, :] = v
        rhs_ref[n, CHUNK:2 * CHUNK, :] = states[hd].astype(BF16)
        states[hd] = cd_ref[hd] * states[hd] + kv
    for hd in range(RET_HEADS):
        state_ref[hd] = states[hd]

    zg = in_proj(c0 + 3 * RET_WIDTH, RET_WIDTH)

    p = cc * ch
    pd1_ref[1:1 + tile, :] = p
    pd2_ref[2:2 + tile, :] = p
    p1 = pd1_ref[0:tile, :]
    p2 = pd2_ref[0:tile, :]
    y = cb * (p2 * convw_ref[0:1, :] + p1 * convw_ref[1:2, :] + p * convw_ref[2:3, :])
    pd1_ref[0:SUBLANES, :] = pd1_ref[tile:tile + SUBLANES, :]
    pd2_ref[0:SUBLANES, :] = pd2_ref[tile:tile + SUBLANES, :]

    lane = lax.broadcasted_iota(jnp.int32, (tile, LANES), 1)
    low = lane < CONV_GROUP_DIM
    for blk in range(CONV_WIDTH // LANES):
        sl = slice(blk * LANES, (blk + 1) * LANES)
        yb = y[:, sl]
        y2 = yb * yb
        ss_lo = jnp.sum(jnp.where(low, y2, 0.0), axis=-1, keepdims=True)
        ss_hi = jnp.sum(jnp.where(low, 0.0, y2), axis=-1, keepdims=True)
        inv = lax.rsqrt(jnp.where(low, ss_lo, ss_hi) * (1.0 / CONV_GROUP_DIM) + NORM_EPS)
        mix_ref[:, sl] = (yb * inv * cng_ref[:, sl]).astype(BF16)

    outs = [jnp.dot(lhs_ref[n], rhs_ref[n], preferred_element_type=F32) for n in range(len(units))]
    for hd in range(RET_HEADS):
        hs = slice(hd * HEAD_DIM, (hd + 1) * HEAD_DIM)
        o = jnp.concatenate([outs[c * RET_HEADS + hd] for c in range(n_chunks)], axis=0)
        gate = zg[:, hs]
        gate = gate * (1.0 / (1.0 + jnp.exp(-gate)))
        yr = o * _rms_scale(o) * rng_ref[:, hs] * gate
        mix_ref[:, CONV_WIDTH + hd * HEAD_DIM:CONV_WIDTH + (hd + 1) * HEAD_DIM] = yr.astype(BF16)

    h_ref[...] = x_ref[...] + jnp.dot(mix_ref[...], wout_ref[...], preferred_element_type=F32)
    h_ref[0:SUBLANES, 0:LANES] += warm[0:SUBLANES, 0:LANES]


def _mlp_kernel(h_ref, g2_ref, wup_ref, wdn_ref, gf_ref, o_ref, hid_ref):
    h = h_ref[...]
    u = (h * _rms_scale(h) * g2_ref[...]).astype(BF16)
    for c in range(D_FF // FF_CHUNK):
        cols = slice(c * FF_CHUNK, (c + 1) * FF_CHUNK)
        a = jnp.maximum(_dot(u, wup_ref[:, cols]), 0.0)
        hid_ref[:, cols] = (a * a).astype(BF16)
    for r in range(h_ref.shape[0] // MLP_DOWN_ROWS):
        rows = slice(r * MLP_DOWN_ROWS, (r + 1) * MLP_DOWN_ROWS)
        y = h_ref[rows, :] + _dot(hid_ref[rows, :], wdn_ref[...])
        o_ref[rows, :] = y * _rms_scale(y) * gf_ref[...]


def _resident(shape):
    nd = len(shape)
    return pl.BlockSpec(shape, lambda *_: (0,) * nd, pipeline_mode=pl.Buffered(1))


@functools.lru_cache(maxsize=None)
def _retention_tables(seq):
    half = HEAD_DIM // 2
    inv_freq = 1.0 / (ROPE_BASE ** (np.arange(half, dtype=np.float64) / half))
    ang = np.arange(seq, dtype=np.float64)[:, None] * inv_freq[None, :]
    cos = np.cos(ang)
    sin = np.sin(ang)
    cos_t = np.concatenate([cos, cos], axis=-1)
    sin_t = np.concatenate([-sin, sin], axis=-1)

    log_gamma = np.log(1.0 - 2.0 ** (-5.0 - np.arange(RET_HEADS, dtype=np.float64)))
    idx = np.arange(CHUNK, dtype=np.float64)
    diff = idx[:, None] - idx[None, :]
    intra = np.where(diff[None] >= 0, np.exp(log_gamma[:, None, None] * np.maximum(diff, 0.0)[None]), 0.0)
    zeta = np.exp(log_gamma[:, None] * (CHUNK - 1 - idx)[None])
    xi = np.exp(log_gamma[:, None] * (idx + 1.0)[None])
    chunk_decay = np.exp(log_gamma * CHUNK)
    k_scale = HEAD_DIM ** -0.5
    decay_t = intra * k_scale
    zeta_t = np.broadcast_to((zeta * k_scale)[:, None, :], (RET_HEADS, HEAD_DIM, CHUNK))
    xi_t = np.broadcast_to(xi[:, :, None], (RET_HEADS, CHUNK, HEAD_DIM))
    return tuple(np.ascontiguousarray(t, dtype=np.float32)
                 for t in (cos_t, sin_t, decay_t, zeta_t, xi_t, chunk_decay))


def kernel(x, norm1_g, w_in, conv_w, conv_norm_g, ret_norm_g, w_out, norm2_g, w_up, w_down, final_norm_g):
    batch, seq, d_model = x.shape
    assert d_model == D_MODEL and w_in.shape == (D_MODEL, IN_COLS)
    assert seq % MIXER_TILE == 0 and MIXER_TILE % CHUNK == 0 and (batch * seq) % MLP_TILE == 0

    cos_t, sin_t, decay_t, zeta_t, xi_t, chunk_decay = _retention_tables(seq)
    row = lambda g: g.reshape(1, -1).astype(F32)
    n_units = (MIXER_TILE // CHUNK) * RET_HEADS

    tile_spec = pl.BlockSpec((None, MIXER_TILE, D_MODEL), lambda b, j: (b, j, 0))
    rope_spec = pl.BlockSpec((MIXER_TILE, HEAD_DIM), lambda b, j: (j, 0))
    seq_tiles = seq // MIXER_TILE
    n_steps = batch * seq_tiles
    assert D_MODEL % n_steps == 0 and D_FF % n_steps == 0
    wup_slab = pl.BlockSpec((D_MODEL // n_steps, D_FF), lambda b, j: (b * seq_tiles + j, 0))
    wdn_slab = pl.BlockSpec((D_FF // n_steps, D_MODEL), lambda b, j: (b * seq_tiles + j, 0))
    h, w_up_bf, w_down_bf = pl.pallas_call(
        _mixer_kernel,
        grid=(batch, seq_tiles),
        in_specs=[
            pl.BlockSpec(memory_space=pltpu.SMEM),
            tile_spec,
            _resident((1, D_MODEL)),
            pl.BlockSpec(memory_space=pl.ANY),
            _resident((CONV_K, CONV_WIDTH)),
            _resident((1, CONV_WIDTH)),
            _resident((1, RET_WIDTH)),
            pl.BlockSpec(memory_space=pl.ANY),
            rope_spec,
            rope_spec,
            _resident((RET_HEADS, CHUNK, CHUNK)),
            _resident((RET_HEADS, CHUNK, HEAD_DIM)),
            _resident((RET_HEADS, CHUNK, HEAD_DIM)),
            wup_slab,
            wdn_slab,
        ],
        out_specs=[tile_spec, wup_slab, wdn_slab],
        out_shape=[jax.ShapeDtypeStruct(x.shape, F32),
                   jax.ShapeDtypeStruct(w_up.shape, BF16),
                   jax.ShapeDtypeStruct(w_down.shape, BF16)],
        scratch_shapes=[
            pltpu.VMEM((D_MODEL, IN_COLS), BF16),
            pltpu.VMEM((D_MODEL, D_MODEL), BF16),
            pltpu.VMEM((CAST_SLOTS, CAST_ROWS, IN_COLS), F32),
            pltpu.VMEM((CAST_SLOTS, CAST_ROWS, D_MODEL), F32),
            pltpu.SemaphoreType.DMA((CAST_SLOTS,)),
            pltpu.SemaphoreType.DMA((CAST_SLOTS,)),
            pltpu.VMEM((RET_HEADS, HEAD_DIM, HEAD_DIM), F32),
            pltpu.VMEM((MIXER_TILE + SUBLANES, CONV_WIDTH), F32),
            pltpu.VMEM((MIXER_TILE + SUBLANES, CONV_WIDTH), F32),
            pltpu.VMEM((MIXER_TILE, D_MODEL), BF16),
            pltpu.VMEM((n_units, CHUNK, 2 * CHUNK), BF16),
            pltpu.VMEM((n_units, 2 * CHUNK, HEAD_DIM), BF16),
        ],
        compiler_params=pltpu.CompilerParams(
            dimension_semantics=("arbitrary", "arbitrary"),
            vmem_limit_bytes=VMEM_LIMIT_BYTES),
        name="mixer",
    )(chunk_decay, x, row(norm1_g), w_in, conv_w, row(conv_norm_g), row(ret_norm_g),
      w_out, cos_t, sin_t, decay_t, zeta_t, xi_t, w_up, w_down)

    tokens = batch * seq
    tok_spec = pl.BlockSpec((MLP_TILE, D_MODEL), lambda i: (i, 0))
    out = pl.pallas_call(
        _mlp_kernel,
        grid=(tokens // MLP_TILE,),
        in_specs=[
            tok_spec,
            _resident((1, D_MODEL)),
            _resident((D_MODEL, D_FF)),
            _resident((D_FF, D_MODEL)),
            _resident((1, D_MODEL)),
        ],
        out_specs=tok_spec,
        out_shape=jax.ShapeDtypeStruct((tokens, D_MODEL), F32),
        scratch_shapes=[pltpu.VMEM((MLP_TILE, D_FF), BF16)],
        compiler_params=pltpu.CompilerParams(
            dimension_semantics=("arbitrary",),
            vmem_limit_bytes=VMEM_LIMIT_BYTES),
        name="mlp",
    )(h.reshape(tokens, D_MODEL), row(norm2_g), w_up_bf, w_down_bf, row(final_norm_g))
    return out.reshape(batch, seq, D_MODEL)
```

```python
import functools

import jax
import jax.numpy as jnp
import numpy as np
from jax import lax
from jax.experimental import pallas as pl
from jax.experimental.pallas import tpu as pltpu

D_MODEL = 1024
CONV_WIDTH = 512
CONV_GROUPS = 8
CONV_GROUP_DIM = CONV_WIDTH // CONV_GROUPS
CONV_K = 3
RET_WIDTH = 512
RET_HEADS = 4
HEAD_DIM = RET_WIDTH // RET_HEADS
CHUNK = 128
ROPE_BASE = 10000.0
D_FF = 4 * D_MODEL
NORM_EPS = 1e-6
IN_COLS = 3 * CONV_WIDTH + 4 * RET_WIDTH

LANES = 128
SUBLANES = 8
VMEM_LIMIT_BYTES = 56 * 1024 * 1024

MIXER_TILE = 1024
MLP_TILE = 1024
MLP_DOWN_ROWS = 256
FF_CHUNK = 1024
WARM_K = 256
CAST_ROWS = 32
CAST_SLOTS = 8

BF16 = jnp.bfloat16
F32 = jnp.float32


def _dot(a, b):
    return jnp.dot(a, b, preferred_element_type=F32)


def _rms_scale(x):
    return lax.rsqrt(jnp.mean(x * x, axis=-1, keepdims=True) + NORM_EPS)


def _bf16_cast_ring(src_hbm, dst_ref, stage_ref, sem_ref):
    slots, rows = stage_ref.shape[0], stage_ref.shape[1]
    n = src_hbm.shape[0] // rows

    def copy(k):
        slot = k % slots
        return pltpu.make_async_copy(src_hbm.at[pl.ds(k * rows, rows), :], stage_ref.at[slot], sem_ref.at[slot])

    def prime():
        for k in range(min(slots, n)):
            copy(k).start()

    def drain():
        for k in range(n):
            copy(k).wait()
            dst_ref[k * rows:(k + 1) * rows, :] = stage_ref[k % slots].astype(BF16)
            if k + slots < n:
                copy(k + slots).start()

    return prime, drain


def _mixer_kernel(cd_ref, x_ref, g1_ref, win_hbm, convw_ref, cng_ref, rng_ref, wout_hbm,
                  cos_ref, sin_ref, decay_ref, zeta_ref, xi_ref, wup_ref, wdn_ref,
                  h_ref, wup_bf_ref, wdn_bf_ref,
                  win_ref, wout_ref, win_stage_ref, wout_stage_ref, win_sem_ref, wout_sem_ref,
                  state_ref, pd1_ref, pd2_ref, mix_ref, lhs_ref, rhs_ref):
    tile = x_ref.shape[0]
    n_chunks = tile // CHUNK
    units = [(c, hd) for c in range(n_chunks) for hd in range(RET_HEADS)]

    @pl.when((pl.program_id(0) == 0) & (pl.program_id(1) == 0))
    def _():
        prime_in, drain_in = _bf16_cast_ring(win_hbm, win_ref, win_stage_ref, win_sem_ref)
        prime_out, drain_out = _bf16_cast_ring(wout_hbm, wout_ref, wout_stage_ref, wout_sem_ref)
        prime_in()
        prime_out()
        drain_in()
        drain_out()

    wup_bf_ref[...] = wup_ref[...].astype(BF16)
    wdn_bf_ref[...] = wdn_ref[...].astype(BF16)

    @pl.when(pl.program_id(1) == 0)
    def _():
        state_ref[...] = jnp.zeros_like(state_ref)
        zeros = jnp.zeros((SUBLANES, CONV_WIDTH), F32)
        pd1_ref[0:SUBLANES, :] = zeros
        pd2_ref[0:SUBLANES, :] = zeros
        pd1_ref[tile:tile + SUBLANES, :] = zeros
        pd2_ref[tile:tile + SUBLANES, :] = zeros

    warm = jnp.dot(jnp.zeros((tile, WARM_K), BF16), win_ref[0:WARM_K, 0:2 * LANES], preferred_element_type=F32)

    x = x_ref[...]
    u = (x * _rms_scale(x) * g1_ref[...]).astype(BF16)

    def in_proj(col0, width):
        return jnp.dot(u, win_ref[:, col0:col0 + width], preferred_element_type=F32)

    c0 = 3 * CONV_WIDTH

    zq = in_proj(c0, RET_WIDTH)
    zk = in_proj(c0 + RET_WIDTH, RET_WIDTH)
    cb = in_proj(0, CONV_WIDTH)
    cc = in_proj(CONV_WIDTH, CONV_WIDTH)

    cos = cos_ref[...]
    sin = sin_ref[...]
    q_rot, kt_rot = [], []
    for hd in range(RET_HEADS):
        hs = slice(hd * HEAD_DIM, (hd + 1) * HEAD_DIM)
        qh = zq[:, hs]
        kh = zk[:, hs]
        q_rot.append(qh * cos + pltpu.roll(qh, HEAD_DIM // 2, axis=1) * sin)
        kt_rot.append((kh * cos + pltpu.roll(kh, HEAD_DIM // 2, axis=1) * sin).T)

    for n, (c, hd) in enumerate(units):
        rows = slice(c * CHUNK, (c + 1) * CHUNK)
        q = q_rot[hd][rows]
        kt = kt_rot[hd][:, rows]
        s = jnp.dot(q.astype(BF16), kt.astype(BF16), preferred_element_type=F32) * decay_ref[hd]
        lhs_ref[n, :, 0:CHUNK] = s.astype(BF16)
        lhs_ref[n, :, CHUNK:2 * CHUNK] = (q * xi_ref[hd]).astype(BF16)

    ch = in_proj(2 * CONV_WIDTH, CONV_WIDTH)
    zv = in_proj(c0 + 2 * RET_WIDTH, RET_WIDTH)
    v_bf = zv.astype(BF16)

    states = [state_ref[hd] for hd in range(RET_HEADS)]
    for n, (c, hd) in enumerate(units):
        rows = slice(c * CHUNK, (c + 1) * CHUNK)
        v = v_bf[rows, hd * HEAD_DIM:(hd + 1) * HEAD_DIM]
        kzt = (kt_rot[hd][:, rows] * zeta_ref[hd]).astype(BF16)
        kv = jnp.dot(kzt, v, preferred_element_type=F32)
        rhs_ref[n, 0:CHUNK, :] = v
        rhs_ref[n, CHUNK:2 * CHUNK, :] = states[hd].astype(BF16)
        states[hd] = cd_ref[hd] * states[hd] + kv
    for hd in range(RET_HEADS):
        state_ref[hd] = states[hd]

    zg = in_proj(c0 + 3 * RET_WIDTH, RET_WIDTH)

    p = cc * ch
    pd1_ref[1:1 + tile, :] = p
    pd2_ref[2:2 + tile, :] = p
    p1 = pd1_ref[0:tile, :]
    p2 = pd2_ref[0:tile, :]
    y = cb * (p2 * convw_ref[0:1, :] + p1 * convw_ref[1:2, :] + p * convw_ref[2:3, :])
    pd1_ref[0:SUBLANES, :] = pd1_ref[tile:tile + SUBLANES, :]
    pd2_ref[0:SUBLANES, :] = pd2_ref[tile:tile + SUBLANES, :]

    lane = lax.broadcasted_iota(jnp.int32, (tile, LANES), 1)
    low = lane < CONV_GROUP_DIM
    for blk in range(CONV_WIDTH // LANES):
        sl = slice(blk * LANES, (blk + 1) * LANES)
        yb = y[:, sl]
        y2 = yb * yb
        ss_lo = jnp.sum(jnp.where(low, y2, 0.0), axis=-1, keepdims=True)
        ss_hi = jnp.sum(jnp.where(low, 0.0, y2), axis=-1, keepdims=True)
        inv = lax.rsqrt(jnp.where(low, ss_lo, ss_hi) * (1.0 / CONV_GROUP_DIM) + NORM_EPS)
        mix_ref[:, sl] = (yb * inv * cng_ref[:, sl]).astype(BF16)

    outs = [jnp.dot(lhs_ref[n], rhs_ref[n], preferred_element_type=F32) for n in range(len(units))]
    for hd in range(RET_HEADS):
        hs = slice(hd * HEAD_DIM, (hd + 1) * HEAD_DIM)
        o = jnp.concatenate([outs[c * RET_HEADS + hd] for c in range(n_chunks)], axis=0)
        gate = zg[:, hs]
        gate = gate * (1.0 / (1.0 + jnp.exp(-gate)))
        yr = o * _rms_scale(o) * rng_ref[:, hs] * gate
        mix_ref[:, CONV_WIDTH + hd * HEAD_DIM:CONV_WIDTH + (hd + 1) * HEAD_DIM] = yr.astype(BF16)

    h_ref[...] = x_ref[...] + jnp.dot(mix_ref[...], wout_ref[...], preferred_element_type=F32)
    h_ref[0:SUBLANES, 0:LANES] += warm[0:SUBLANES, 0:LANES]


def _mlp_kernel(h_ref, g2_ref, wup_ref, wdn_ref, gf_ref, o_ref, hid_ref):
    warm = jnp.dot(jnp.zeros((h_ref.shape[0], WARM_K), BF16), wup_ref[0:WARM_K, 0:2 * LANES],
                   preferred_element_type=F32)
    h = h_ref[...]
    u = (h * _rms_scale(h) * g2_ref[...]).astype(BF16)
    for c in range(D_FF // FF_CHUNK):
        cols = slice(c * FF_CHUNK, (c + 1) * FF_CHUNK)
        a = jnp.maximum(_dot(u, wup_ref[:, cols]), 0.0)
        hid_ref[:, cols] = (a * a).astype(BF16)
    for r in range(h_ref.shape[0] // MLP_DOWN_ROWS):
        rows = slice(r * MLP_DOWN_ROWS, (r + 1) * MLP_DOWN_ROWS)
        y = h_ref[rows, :] + _dot(hid_ref[rows, :], wdn_ref[...])
        o_ref[rows, :] = y * _rms_scale(y) * gf_ref[...]
    o_ref[0:SUBLANES, 0:LANES] += warm[0:SUBLANES, 0:LANES]


def _resident(shape):
    nd = len(shape)
    return pl.BlockSpec(shape, lambda *_: (0,) * nd, pipeline_mode=pl.Buffered(1))


@functools.lru_cache(maxsize=None)
def _retention_tables(seq):
    half = HEAD_DIM // 2
    inv_freq = 1.0 / (ROPE_BASE ** (np.arange(half, dtype=np.float64) / half))
    ang = np.arange(seq, dtype=np.float64)[:, None] * inv_freq[None, :]
    cos = np.cos(ang)
    sin = np.sin(ang)
    cos_t = np.concatenate([cos, cos], axis=-1)
    sin_t = np.concatenate([-sin, sin], axis=-1)

    log_gamma = np.log(1.0 - 2.0 ** (-5.0 - np.arange(RET_HEADS, dtype=np.float64)))
    idx = np.arange(CHUNK, dtype=np.float64)
    diff = idx[:, None] - idx[None, :]
    intra = np.where(diff[None] >= 0, np.exp(log_gamma[:, None, None] * np.maximum(diff, 0.0)[None]), 0.0)
    zeta = np.exp(log_gamma[:, None] * (CHUNK - 1 - idx)[None])
    xi = np.exp(log_gamma[:, None] * (idx + 1.0)[None])
    chunk_decay = np.exp(log_gamma * CHUNK)
    k_scale = HEAD_DIM ** -0.5
    decay_t = intra * k_scale
    zeta_t = np.broadcast_to((zeta * k_scale)[:, None, :], (RET_HEADS, HEAD_DIM, CHUNK))
    xi_t = np.broadcast_to(xi[:, :, None], (RET_HEADS, CHUNK, HEAD_DIM))
    return tuple(np.ascontiguousarray(t, dtype=np.float32)
                 for t in (cos_t, sin_t, decay_t, zeta_t, xi_t, chunk_decay))


def kernel(x, norm1_g, w_in, conv_w, conv_norm_g, ret_norm_g, w_out, norm2_g, w_up, w_down, final_norm_g):
    batch, seq, d_model = x.shape
    assert d_model == D_MODEL and w_in.shape == (D_MODEL, IN_COLS)
    assert seq % MIXER_TILE == 0 and MIXER_TILE % CHUNK == 0 and (batch * seq) % MLP_TILE == 0

    cos_t, sin_t, decay_t, zeta_t, xi_t, chunk_decay = _retention_tables(seq)
    row = lambda g: g.reshape(1, -1).astype(F32)
    n_units = (MIXER_TILE // CHUNK) * RET_HEADS

    tile_spec = pl.BlockSpec((None, MIXER_TILE, D_MODEL), lambda b, j: (b, j, 0))
    rope_spec = pl.BlockSpec((MIXER_TILE, HEAD_DIM), lambda b, j: (j, 0))
    seq_tiles = seq // MIXER_TILE
    n_steps = batch * seq_tiles
    assert D_MODEL % n_steps == 0 and D_FF % n_steps == 0
    wup_slab = pl.BlockSpec((D_MODEL // n_steps, D_FF), lambda b, j: (b * seq_tiles + j, 0))
    wdn_slab = pl.BlockSpec((D_FF // n_steps, D_MODEL), lambda b, j: (b * seq_tiles + j, 0))
    h, w_up_bf, w_down_bf = pl.pallas_call(
        _mixer_kernel,
        grid=(batch, seq_tiles),
        in_specs=[
            pl.BlockSpec(memory_space=pltpu.SMEM),
            tile_spec,
            _resident((1, D_MODEL)),
            pl.BlockSpec(memory_space=pl.ANY),
            _resident((CONV_K, CONV_WIDTH)),
            _resident((1, CONV_WIDTH)),
            _resident((1, RET_WIDTH)),
            pl.BlockSpec(memory_space=pl.ANY),
            rope_spec,
            rope_spec,
            _resident((RET_HEADS, CHUNK, CHUNK)),
            _resident((RET_HEADS, CHUNK, HEAD_DIM)),
            _resident((RET_HEADS, CHUNK, HEAD_DIM)),
            wup_slab,
            wdn_slab,
        ],
        out_specs=[tile_spec, wup_slab, wdn_slab],
        out_shape=[jax.ShapeDtypeStruct(x.shape, F32),
                   jax.ShapeDtypeStruct(w_up.shape, BF16),
                   jax.ShapeDtypeStruct(w_down.shape, BF16)],
        scratch_shapes=[
            pltpu.VMEM((D_MODEL, IN_COLS), BF16),
            pltpu.VMEM((D_MODEL, D_MODEL), BF16),
            pltpu.VMEM((CAST_SLOTS, CAST_ROWS, IN_COLS), F32),
            pltpu.VMEM((CAST_SLOTS, CAST_ROWS, D_MODEL), F32),
            pltpu.SemaphoreType.DMA((CAST_SLOTS,)),
            pltpu.SemaphoreType.DMA((CAST_SLOTS,)),
            pltpu.VMEM((RET_HEADS, HEAD_DIM, HEAD_DIM), F32),
            pltpu.VMEM((MIXER_TILE + SUBLANES, CONV_WIDTH), F32),
            pltpu.VMEM((MIXER_TILE + SUBLANES, CONV_WIDTH), F32),
            pltpu.VMEM((MIXER_TILE, D_MODEL), BF16),
            pltpu.VMEM((n_units, CHUNK, 2 * CHUNK), BF16),
            pltpu.VMEM((n_units, 2 * CHUNK, HEAD_DIM), BF16),
        ],
        compiler_params=pltpu.CompilerParams(
            dimension_semantics=("arbitrary", "arbitrary"),
            vmem_limit_bytes=VMEM_LIMIT_BYTES),
        name="mixer",
    )(chunk_decay, x, row(norm1_g), w_in, conv_w, row(conv_norm_g), row(ret_norm_g),
      w_out, cos_t, sin_t, decay_t, zeta_t, xi_t, w_up, w_down)

    tokens = batch * seq
    tok_spec = pl.BlockSpec((MLP_TILE, D_MODEL), lambda i: (i, 0))
    out = pl.pallas_call(
        _mlp_kernel,
        grid=(tokens // MLP_TILE,),
        in_specs=[
            tok_spec,
            _resident((1, D_MODEL)),
            _resident((D_MODEL, D_FF)),
            _resident((D_FF, D_MODEL)),
            _resident((1, D_MODEL)),
        ],
        out_specs=tok_spec,
        out_shape=jax.ShapeDtypeStruct((tokens, D_MODEL), F32),
        scratch_shapes=[pltpu.VMEM((MLP_TILE, D_FF), BF16)],
        compiler_params=pltpu.CompilerParams(
            dimension_semantics=("arbitrary",),
            vmem_limit_bytes=VMEM_LIMIT_BYTES),
        name="mlp",
    )(h.reshape(tokens, D_MODEL), row(norm2_g), w_up_bf, w_down_bf, row(final_norm_g))
    return out.reshape(batch, seq, D_MODEL)
```

```python
import functools

import jax
import jax.numpy as jnp
import numpy as np
from jax import lax
from jax.experimental import pallas as pl
from jax.experimental.pallas import tpu as pltpu

D_MODEL = 1024
CONV_WIDTH = 512
CONV_GROUPS = 8
CONV_GROUP_DIM = CONV_WIDTH // CONV_GROUPS
CONV_K = 3
RET_WIDTH = 512
RET_HEADS = 4
HEAD_DIM = RET_WIDTH // RET_HEADS
CHUNK = 128
ROPE_BASE = 10000.0
D_FF = 4 * D_MODEL
NORM_EPS = 1e-6
IN_COLS = 3 * CONV_WIDTH + 4 * RET_WIDTH

LANES = 128
SUBLANES = 8
VMEM_LIMIT_BYTES = 56 * 1024 * 1024

MIXER_TILE = 1024
MLP_TILE = 1024
MLP_DOWN_ROWS = 256
FF_CHUNK = 1024
WARM_K = 256
CAST_ROWS = 32
CAST_SLOTS = 8

BF16 = jnp.bfloat16
F32 = jnp.float32


def _dot(a, b):
    return jnp.dot(a, b, preferred_element_type=F32)


def _rms_scale(x):
    return lax.rsqrt(jnp.mean(x * x, axis=-1, keepdims=True) + NORM_EPS)


def _bf16_cast_ring(src_hbm, dst_ref, stage_ref, sem_ref):
    slots, rows = stage_ref.shape[0], stage_ref.shape[1]
    n = src_hbm.shape[0] // rows

    def copy(k):
        slot = k % slots
        return pltpu.make_async_copy(src_hbm.at[pl.ds(k * rows, rows), :], stage_ref.at[slot], sem_ref.at[slot])

    def prime():
        for k in range(min(slots, n)):
            copy(k).start()

    def drain():
        for k in range(n):
            copy(k).wait()
            dst_ref[k * rows:(k + 1) * rows, :] = stage_ref[k % slots].astype(BF16)
            if k + slots < n:
                copy(k + slots).start()

    return prime, drain


def _mixer_kernel(cd_ref, x_ref, g1_ref, win_hbm, convw_ref, cng_ref, rng_ref, wout_hbm,
                  cos_ref, sin_ref, decay_ref, zeta_ref, xi_ref, wup_ref, wdn_ref,
                  h_ref, wup_bf_ref, wdn_bf_ref,
                  win_ref, wout_ref, win_stage_ref, wout_stage_ref, win_sem_ref, wout_sem_ref,
                  state_ref, pd1_ref, pd2_ref, mix_ref, lhs_ref, rhs_ref):
    tile = x_ref.shape[0]
    n_chunks = tile // CHUNK
    units = [(c, hd) for c in range(n_chunks) for hd in range(RET_HEADS)]

    @pl.when((pl.program_id(0) == 0) & (pl.program_id(1) == 0))
    def _():
        prime_in, drain_in = _bf16_cast_ring(win_hbm, win_ref, win_stage_ref, win_sem_ref)
        prime_out, drain_out = _bf16_cast_ring(wout_hbm, wout_ref, wout_stage_ref, wout_sem_ref)
        prime_in()
        prime_out()
        drain_in()
        drain_out()

    wup_bf_ref[...] = wup_ref[...].astype(BF16)
    wdn_bf_ref[...] = wdn_ref[...].astype(BF16)

    @pl.when(pl.program_id(1) == 0)
    def _():
        state_ref[...] = jnp.zeros_like(state_ref)
        zeros = jnp.zeros((SUBLANES, CONV_WIDTH), F32)
        pd1_ref[0:SUBLANES, :] = zeros
        pd2_ref[0:SUBLANES, :] = zeros
        pd1_ref[tile:tile + SUBLANES, :] = zeros
        pd2_ref[tile:tile + SUBLANES, :] = zeros

    warm = jnp.dot(jnp.zeros((tile // 2, WARM_K), BF16), win_ref[0:WARM_K, 0:2 * LANES],
                   preferred_element_type=F32)

    x = x_ref[...]
    u = (x * _rms_scale(x) * g1_ref[...]).astype(BF16)

    def in_proj(col0, width):
        return jnp.dot(u, win_ref[:, col0:col0 + width], preferred_element_type=F32)

    c0 = 3 * CONV_WIDTH

    zq = in_proj(c0, RET_WIDTH)
    zk = in_proj(c0 + RET_WIDTH, RET_WIDTH)
    cb = in_proj(0, CONV_WIDTH)
    cc = in_proj(CONV_WIDTH, CONV_WIDTH)

    cos = cos_ref[...]
    sin = sin_ref[...]
    q_rot, kt_rot = [], []
    for hd in range(RET_HEADS):
        hs = slice(hd * HEAD_DIM, (hd + 1) * HEAD_DIM)
        qh = zq[:, hs]
        kh = zk[:, hs]
        q_rot.append(qh * cos + pltpu.roll(qh, HEAD_DIM // 2, axis=1) * sin)
        kt_rot.append((kh * cos + pltpu.roll(kh, HEAD_DIM // 2, axis=1) * sin).T)

    for n, (c, hd) in enumerate(units):
        rows = slice(c * CHUNK, (c + 1) * CHUNK)
        q = q_rot[hd][rows]
        kt = kt_rot[hd][:, rows]
        s = jnp.dot(q.astype(BF16), kt.astype(BF16), preferred_element_type=F32) * decay_ref[hd]
        lhs_ref[n, :, 0:CHUNK] = s.astype(BF16)
        lhs_ref[n, :, CHUNK:2 * CHUNK] = (q * xi_ref[hd]).astype(BF16)

    ch = in_proj(2 * CONV_WIDTH, CONV_WIDTH)
    zv = in_proj(c0 + 2 * RET_WIDTH, RET_WIDTH)
    v_bf = zv.astype(BF16)

    states = [state_ref[hd] for hd in range(RET_HEADS)]
    for n, (c, hd) in enumerate(units):
        rows = slice(c * CHUNK, (c + 1) * CHUNK)
        v = v_bf[rows, hd * HEAD_DIM:(hd + 1) * HEAD_DIM]
        kzt = (kt_rot[hd][:, rows] * zeta_ref[hd]).astype(BF16)
        kv = jnp.dot(kzt, v, preferred_element_type=F32)
        rhs_ref[n, 0:CHUNK, :] = v
        rhs_ref[n, CHUNK:2 * CHUNK, :] = states[hd].astype(BF16)
        states[hd] = cd_ref[hd] * states[hd] + kv
    for hd in range(RET_HEADS):
        state_ref[hd] = states[hd]

    zg = in_proj(c0 + 3 * RET_WIDTH, RET_WIDTH)

    p = cc * ch
    pd1_ref[1:1 + tile, :] = p
    pd2_ref[2:2 + tile, :] = p
    p1 = pd1_ref[0:tile, :]
    p2 = pd2_ref[0:tile, :]
    y = cb * (p2 * convw_ref[0:1, :] + p1 * convw_ref[1:2, :] + p * convw_ref[2:3, :])
    pd1_ref[0:SUBLANES, :] = pd1_ref[tile:tile + SUBLANES, :]
    pd2_ref[0:SUBLANES, :] = pd2_ref[tile:tile + SUBLANES, :]

    lane = lax.broadcasted_iota(jnp.int32, (tile, LANES), 1)
    low = lane < CONV_GROUP_DIM
    for blk in range(CONV_WIDTH // LANES):
        sl = slice(blk * LANES, (blk + 1) * LANES)
        yb = y[:, sl]
        y2 = yb * yb
        ss_lo = jnp.sum(jnp.where(low, y2, 0.0), axis=-1, keepdims=True)
        ss_hi = jnp.sum(jnp.where(low, 0.0, y2), axis=-1, keepdims=True)
        inv = lax.rsqrt(jnp.where(low, ss_lo, ss_hi) * (1.0 / CONV_GROUP_DIM) + NORM_EPS)
        mix_ref[:, sl] = (yb * inv * cng_ref[:, sl]).astype(BF16)

    outs = [jnp.dot(lhs_ref[n], rhs_ref[n], preferred_element_type=F32) for n in range(len(units))]
    for hd in range(RET_HEADS):
        hs = slice(hd * HEAD_DIM, (hd + 1) * HEAD_DIM)
        o = jnp.concatenate([outs[c * RET_HEADS + hd] for c in range(n_chunks)], axis=0)
        gate = zg[:, hs]
        gate = gate * (1.0 / (1.0 + jnp.exp(-gate)))
        yr = o * _rms_scale(o) * rng_ref[:, hs] * gate
        mix_ref[:, CONV_WIDTH + hd * HEAD_DIM:CONV_WIDTH + (hd + 1) * HEAD_DIM] = yr.astype(BF16)

    h_ref[...] = x_ref[...] + jnp.dot(mix_ref[...], wout_ref[...], preferred_element_type=F32)
    h_ref[0:SUBLANES, 0:LANES] += warm[0:SUBLANES, 0:LANES]


def _mlp_kernel(h_ref, g2_ref, wup_ref, wdn_ref, gf_ref, o_ref, hid_ref):
    warm = jnp.dot(jnp.zeros((h_ref.shape[0], WARM_K), BF16), wup_ref[0:WARM_K, 0:2 * LANES],
                   preferred_element_type=F32)
    h = h_ref[...]
    u = (h * _rms_scale(h) * g2_ref[...]).astype(BF16)
    for c in range(D_FF // FF_CHUNK):
        cols = slice(c * FF_CHUNK, (c + 1) * FF_CHUNK)
        a = jnp.maximum(_dot(u, wup_ref[:, cols]), 0.0)
        hid_ref[:, cols] = (a * a).astype(BF16)
    for r in range(h_ref.shape[0] // MLP_DOWN_ROWS):
        rows = slice(r * MLP_DOWN_ROWS, (r + 1) * MLP_DOWN_ROWS)
        y = h_ref[rows, :] + _dot(hid_ref[rows, :], wdn_ref[...])
        o_ref[rows, :] = y * _rms_scale(y) * gf_ref[...]
    o_ref[0:SUBLANES, 0:LANES] += warm[0:SUBLANES, 0:LANES]


def _resident(shape):
    nd = len(shape)
    return pl.BlockSpec(shape, lambda *_: (0,) * nd, pipeline_mode=pl.Buffered(1))


@functools.lru_cache(maxsize=None)
def _retention_tables(seq):
    half = HEAD_DIM // 2
    inv_freq = 1.0 / (ROPE_BASE ** (np.arange(half, dtype=np.float64) / half))
    ang = np.arange(seq, dtype=np.float64)[:, None] * inv_freq[None, :]
    cos = np.cos(ang)
    sin = np.sin(ang)
    cos_t = np.concatenate([cos, cos], axis=-1)
    sin_t = np.concatenate([-sin, sin], axis=-1)

    log_gamma = np.log(1.0 - 2.0 ** (-5.0 - np.arange(RET_HEADS, dtype=np.float64)))
    idx = np.arange(CHUNK, dtype=np.float64)
    diff = idx[:, None] - idx[None, :]
    intra = np.where(diff[None] >= 0, np.exp(log_gamma[:, None, None] * np.maximum(diff, 0.0)[None]), 0.0)
    zeta = np.exp(log_gamma[:, None] * (CHUNK - 1 - idx)[None])
    xi = np.exp(log_gamma[:, None] * (idx + 1.0)[None])
    chunk_decay = np.exp(log_gamma * CHUNK)
    k_scale = HEAD_DIM ** -0.5
    decay_t = intra * k_scale
    zeta_t = np.broadcast_to((zeta * k_scale)[:, None, :], (RET_HEADS, HEAD_DIM, CHUNK))
    xi_t = np.broadcast_to(xi[:, :, None], (RET_HEADS, CHUNK, HEAD_DIM))
    return tuple(np.ascontiguousarray(t, dtype=np.float32)
                 for t in (cos_t, sin_t, decay_t, zeta_t, xi_t, chunk_decay))


def kernel(x, norm1_g, w_in, conv_w, conv_norm_g, ret_norm_g, w_out, norm2_g, w_up, w_down, final_norm_g):
    batch, seq, d_model = x.shape
    assert d_model == D_MODEL and w_in.shape == (D_MODEL, IN_COLS)
    assert seq % MIXER_TILE == 0 and MIXER_TILE % CHUNK == 0 and (batch * seq) % MLP_TILE == 0

    cos_t, sin_t, decay_t, zeta_t, xi_t, chunk_decay = _retention_tables(seq)
    row = lambda g: g.reshape(1, -1).astype(F32)
    n_units = (MIXER_TILE // CHUNK) * RET_HEADS

    tile_spec = pl.BlockSpec((None, MIXER_TILE, D_MODEL), lambda b, j: (b, j, 0))
    rope_spec = pl.BlockSpec((MIXER_TILE, HEAD_DIM), lambda b, j: (j, 0))
    seq_tiles = seq // MIXER_TILE
    n_steps = batch * seq_tiles
    assert D_MODEL % n_steps == 0 and D_FF % n_steps == 0
    wup_slab = pl.BlockSpec((D_MODEL // n_steps, D_FF), lambda b, j: (b * seq_tiles + j, 0))
    wdn_slab = pl.BlockSpec((D_FF // n_steps, D_MODEL), lambda b, j: (b * seq_tiles + j, 0))
    h, w_up_bf, w_down_bf = pl.pallas_call(
        _mixer_kernel,
        grid=(batch, seq_tiles),
        in_specs=[
            pl.BlockSpec(memory_space=pltpu.SMEM),
            tile_spec,
            _resident((1, D_MODEL)),
            pl.BlockSpec(memory_space=pl.ANY),
            _resident((CONV_K, CONV_WIDTH)),
            _resident((1, CONV_WIDTH)),
            _resident((1, RET_WIDTH)),
            pl.BlockSpec(memory_space=pl.ANY),
            rope_spec,
            rope_spec,
            _resident((RET_HEADS, CHUNK, CHUNK)),
            _resident((RET_HEADS, CHUNK, HEAD_DIM)),
            _resident((RET_HEADS, CHUNK, HEAD_DIM)),
            wup_slab,
            wdn_slab,
        ],
        out_specs=[tile_spec, wup_slab, wdn_slab],
        out_shape=[jax.ShapeDtypeStruct(x.shape, F32),
                   jax.ShapeDtypeStruct(w_up.shape, BF16),
                   jax.ShapeDtypeStruct(w_down.shape, BF16)],
        scratch_shapes=[
            pltpu.VMEM((D_MODEL, IN_COLS), BF16),
            pltpu.VMEM((D_MODEL, D_MODEL), BF16),
            pltpu.VMEM((CAST_SLOTS, CAST_ROWS, IN_COLS), F32),
            pltpu.VMEM((CAST_SLOTS, CAST_ROWS, D_MODEL), F32),
            pltpu.SemaphoreType.DMA((CAST_SLOTS,)),
            pltpu.SemaphoreType.DMA((CAST_SLOTS,)),
            pltpu.VMEM((RET_HEADS, HEAD_DIM, HEAD_DIM), F32),
            pltpu.VMEM((MIXER_TILE + SUBLANES, CONV_WIDTH), F32),
            pltpu.VMEM((MIXER_TILE + SUBLANES, CONV_WIDTH), F32),
            pltpu.VMEM((MIXER_TILE, D_MODEL), BF16),
            pltpu.VMEM((n_units, CHUNK, 2 * CHUNK), BF16),
            pltpu.VMEM((n_units, 2 * CHUNK, HEAD_DIM), BF16),
        ],
        compiler_params=pltpu.CompilerParams(
            dimension_semantics=("arbitrary", "arbitrary"),
            vmem_limit_bytes=VMEM_LIMIT_BYTES),
        name="mixer",
    )(chunk_decay, x, row(norm1_g), w_in, conv_w, row(conv_norm_g), row(ret_norm_g),
      w_out, cos_t, sin_t, decay_t, zeta_t, xi_t, w_up, w_down)

    tokens = batch * seq
    tok_spec = pl.BlockSpec((MLP_TILE, D_MODEL), lambda i: (i, 0))
    out = pl.pallas_call(
        _mlp_kernel,
        grid=(tokens // MLP_TILE,),
        in_specs=[
            tok_spec,
            _resident((1, D_MODEL)),
            _resident((D_MODEL, D_FF)),
            _resident((D_FF, D_MODEL)),
            _resident((1, D_MODEL)),
        ],
        out_specs=tok_spec,
        out_shape=jax.ShapeDtypeStruct((tokens, D_MODEL), F32),
        scratch_shapes=[pltpu.VMEM((MLP_TILE, D_FF), BF16)],
        compiler_params=pltpu.CompilerParams(
            dimension_semantics=("arbitrary",),
            vmem_limit_bytes=VMEM_LIMIT_BYTES),
        name="mlp",
    )(h.reshape(tokens, D_MODEL), row(norm2_g), w_up_bf, w_down_bf, row(final_norm_g))
    return out.reshape(batch, seq, D_MODEL)
```

```python
import functools

import jax
import jax.numpy as jnp
import numpy as np
from jax import lax
from jax.experimental import pallas as pl
from jax.experimental.pallas import tpu as pltpu

D_MODEL = 1024
CONV_WIDTH = 512
CONV_GROUPS = 8
CONV_GROUP_DIM = CONV_WIDTH // CONV_GROUPS
CONV_K = 3
RET_WIDTH = 512
RET_HEADS = 4
HEAD_DIM = RET_WIDTH // RET_HEADS
CHUNK = 128
ROPE_BASE = 10000.0
D_FF = 4 * D_MODEL
NORM_EPS = 1e-6
IN_COLS = 3 * CONV_WIDTH + 4 * RET_WIDTH

LANES = 128
SUBLANES = 8
VMEM_LIMIT_BYTES = 56 * 1024 * 1024

MIXER_TILE = 1024
MLP_TILE = 1024
MLP_DOWN_ROWS = 256
FF_CHUNK = 1024
WARM_K = 256
CAST_ROWS = 32
CAST_SLOTS = 8

BF16 = jnp.bfloat16
F32 = jnp.float32


def _dot(a, b):
    return jnp.dot(a, b, preferred_element_type=F32)


def _rms_scale(x):
    return lax.rsqrt(jnp.mean(x * x, axis=-1, keepdims=True) + NORM_EPS)


def _bf16_cast_ring(src_hbm, dst_ref, stage_ref, sem_ref):
    slots, rows = stage_ref.shape[0], stage_ref.shape[1]
    n = src_hbm.shape[0] // rows

    def copy(k):
        slot = k % slots
        return pltpu.make_async_copy(src_hbm.at[pl.ds(k * rows, rows), :], stage_ref.at[slot], sem_ref.at[slot])

    def prime():
        for k in range(min(slots, n)):
            copy(k).start()

    def drain():
        for k in range(n):
            copy(k).wait()
            dst_ref[k * rows:(k + 1) * rows, :] = stage_ref[k % slots].astype(BF16)
            if k + slots < n:
                copy(k + slots).start()

    return prime, drain


def _mixer_kernel(cd_ref, x_ref, g1_ref, win_hbm, convw_ref, cng_ref, rng_ref, wout_hbm,
                  cos_ref, sin_ref, decay_ref, zeta_ref, xi_ref, wup_ref, wdn_ref,
                  h_ref, wup_bf_ref, wdn_bf_ref,
                  win_ref, wout_ref, win_stage_ref, wout_stage_ref, win_sem_ref, wout_sem_ref,
                  state_ref, pd1_ref, pd2_ref, mix_ref, lhs_ref, rhs_ref):
    tile = x_ref.shape[0]
    n_chunks = tile // CHUNK
    units = [(c, hd) for c in range(n_chunks) for hd in range(RET_HEADS)]

    @pl.when((pl.program_id(0) == 0) & (pl.program_id(1) == 0))
    def _():
        prime_in, drain_in = _bf16_cast_ring(win_hbm, win_ref, win_stage_ref, win_sem_ref)
        prime_out, drain_out = _bf16_cast_ring(wout_hbm, wout_ref, wout_stage_ref, wout_sem_ref)
        prime_in()
        prime_out()
        drain_in()
        drain_out()

    wup_bf_ref[...] = wup_ref[...].astype(BF16)
    wdn_bf_ref[...] = wdn_ref[...].astype(BF16)

    @pl.when(pl.program_id(1) == 0)
    def _():
        state_ref[...] = jnp.zeros_like(state_ref)
        zeros = jnp.zeros((SUBLANES, CONV_WIDTH), F32)
        pd1_ref[0:SUBLANES, :] = zeros
        pd2_ref[0:SUBLANES, :] = zeros
        pd1_ref[tile:tile + SUBLANES, :] = zeros
        pd2_ref[tile:tile + SUBLANES, :] = zeros

    warm = jnp.dot(jnp.zeros((tile // 4, WARM_K), BF16), win_ref[0:WARM_K, 0:2 * LANES],
                   preferred_element_type=F32)

    x = x_ref[...]
    u = (x * _rms_scale(x) * g1_ref[...]).astype(BF16)

    def in_proj(col0, width):
        return jnp.dot(u, win_ref[:, col0:col0 + width], preferred_element_type=F32)

    c0 = 3 * CONV_WIDTH

    zq = in_proj(c0, RET_WIDTH)
    zk = in_proj(c0 + RET_WIDTH, RET_WIDTH)
    cb = in_proj(0, CONV_WIDTH)
    cc = in_proj(CONV_WIDTH, CONV_WIDTH)

    cos = cos_ref[...]
    sin = sin_ref[...]
    q_rot, kt_rot = [], []
    for hd in range(RET_HEADS):
        hs = slice(hd * HEAD_DIM, (hd + 1) * HEAD_DIM)
        qh = zq[:, hs]
        kh = zk[:, hs]
        q_rot.append(qh * cos + pltpu.roll(qh, HEAD_DIM // 2, axis=1) * sin)
        kt_rot.append((kh * cos + pltpu.roll(kh, HEAD_DIM // 2, axis=1) * sin).T)

    for n, (c, hd) in enumerate(units):
        rows = slice(c * CHUNK, (c + 1) * CHUNK)
        q = q_rot[hd][rows]
        kt = kt_rot[hd][:, rows]
        s = jnp.dot(q.astype(BF16), kt.astype(BF16), preferred_element_type=F32) * decay_ref[hd]
        lhs_ref[n, :, 0:CHUNK] = s.astype(BF16)
        lhs_ref[n, :, CHUNK:2 * CHUNK] = (q * xi_ref[hd]).astype(BF16)

    ch = in_proj(2 * CONV_WIDTH, CONV_WIDTH)
    zv = in_proj(c0 + 2 * RET_WIDTH, RET_WIDTH)
    v_bf = zv.astype(BF16)

    states = [state_ref[hd] for hd in range(RET_HEADS)]
    for n, (c, hd) in enumerate(units):
        rows = slice(c * CHUNK, (c + 1) * CHUNK)
        v = v_bf[rows, hd * HEAD_DIM:(hd + 1) * HEAD_DIM]
        kzt = (kt_rot[hd][:, rows] * zeta_ref[hd]).astype(BF16)
        kv = jnp.dot(kzt, v, preferred_element_type=F32)
        rhs_ref[n, 0:CHUNK, :] = v
        rhs_ref[n, CHUNK:2 * CHUNK, :] = states[hd].astype(BF16)
        states[hd] = cd_ref[hd] * states[hd] + kv
    for hd in range(RET_HEADS):
        state_ref[hd] = states[hd]

    zg = in_proj(c0 + 3 * RET_WIDTH, RET_WIDTH)

    p = cc * ch
    pd1_ref[1:1 + tile, :] = p
    pd2_ref[2:2 + tile, :] = p
    p1 = pd1_ref[0:tile, :]
    p2 = pd2_ref[0:tile, :]
    y = cb * (p2 * convw_ref[0:1, :] + p1 * convw_ref[1:2, :] + p * convw_ref[2:3, :])
    pd1_ref[0:SUBLANES, :] = pd1_ref[tile:tile + SUBLANES, :]
    pd2_ref[0:SUBLANES, :] = pd2_ref[tile:tile + SUBLANES, :]

    lane = lax.broadcasted_iota(jnp.int32, (tile, LANES), 1)
    low = lane < CONV_GROUP_DIM
    for blk in range(CONV_WIDTH // LANES):
        sl = slice(blk * LANES, (blk + 1) * LANES)
        yb = y[:, sl]
        y2 = yb * yb
        ss_lo = jnp.sum(jnp.where(low, y2, 0.0), axis=-1, keepdims=True)
        ss_hi = jnp.sum(jnp.where(low, 0.0, y2), axis=-1, keepdims=True)
        inv = lax.rsqrt(jnp.where(low, ss_lo, ss_hi) * (1.0 / CONV_GROUP_DIM) + NORM_EPS)
        mix_ref[:, sl] = (yb * inv * cng_ref[:, sl]).astype(BF16)

    outs = [jnp.dot(lhs_ref[n], rhs_ref[n], preferred_element_type=F32) for n in range(len(units))]
    for hd in range(RET_HEADS):
        hs = slice(hd * HEAD_DIM, (hd + 1) * HEAD_DIM)
        o = jnp.concatenate([outs[c * RET_HEADS + hd] for c in range(n_chunks)], axis=0)
        gate = zg[:, hs]
        gate = gate * (1.0 / (1.0 + jnp.exp(-gate)))
        yr = o * _rms_scale(o) * rng_ref[:, hs] * gate
        mix_ref[:, CONV_WIDTH + hd * HEAD_DIM:CONV_WIDTH + (hd + 1) * HEAD_DIM] = yr.astype(BF16)

    h_ref[...] = x_ref[...] + jnp.dot(mix_ref[...], wout_ref[...], preferred_element_type=F32)
    h_ref[0:SUBLANES, 0:LANES] += warm[0:SUBLANES, 0:LANES]


def _mlp_kernel(h_ref, g2_ref, wup_ref, wdn_ref, gf_ref, o_ref, hid_ref):
    warm = jnp.dot(jnp.zeros((h_ref.shape[0] // 2, WARM_K), BF16), wup_ref[0:WARM_K, 0:2 * LANES],
                   preferred_element_type=F32)
    h = h_ref[...]
    u = (h * _rms_scale(h) * g2_ref[...]).astype(BF16)
    for c in range(D_FF // FF_CHUNK):
        cols = slice(c * FF_CHUNK, (c + 1) * FF_CHUNK)
        a = jnp.maximum(_dot(u, wup_ref[:, cols]), 0.0)
        hid_ref[:, cols] = (a * a).astype(BF16)
    for r in range(h_ref.shape[0] // MLP_DOWN_ROWS):
        rows = slice(r * MLP_DOWN_ROWS, (r + 1) * MLP_DOWN_ROWS)
        y = h_ref[rows, :] + _dot(hid_ref[rows, :], wdn_ref[...])
        o_ref[rows, :] = y * _rms_scale(y) * gf_ref[...]
    o_ref[0:SUBLANES, 0:LANES] += warm[0:SUBLANES, 0:LANES]


def _resident(shape):
    nd = len(shape)
    return pl.BlockSpec(shape, lambda *_: (0,) * nd, pipeline_mode=pl.Buffered(1))


@functools.lru_cache(maxsize=None)
def _retention_tables(seq):
    half = HEAD_DIM // 2
    inv_freq = 1.0 / (ROPE_BASE ** (np.arange(half, dtype=np.float64) / half))
    ang = np.arange(seq, dtype=np.float64)[:, None] * inv_freq[None, :]
    cos = np.cos(ang)
    sin = np.sin(ang)
    cos_t = np.concatenate([cos, cos], axis=-1)
    sin_t = np.concatenate([-sin, sin], axis=-1)

    log_gamma = np.log(1.0 - 2.0 ** (-5.0 - np.arange(RET_HEADS, dtype=np.float64)))
    idx = np.arange(CHUNK, dtype=np.float64)
    diff = idx[:, None] - idx[None, :]
    intra = np.where(diff[None] >= 0, np.exp(log_gamma[:, None, None] * np.maximum(diff, 0.0)[None]), 0.0)
    zeta = np.exp(log_gamma[:, None] * (CHUNK - 1 - idx)[None])
    xi = np.exp(log_gamma[:, None] * (idx + 1.0)[None])
    chunk_decay = np.exp(log_gamma * CHUNK)
    k_scale = HEAD_DIM ** -0.5
    decay_t = intra * k_scale
    zeta_t = np.broadcast_to((zeta * k_scale)[:, None, :], (RET_HEADS, HEAD_DIM, CHUNK))
    xi_t = np.broadcast_to(xi[:, :, None], (RET_HEADS, CHUNK, HEAD_DIM))
    return tuple(np.ascontiguousarray(t, dtype=np.float32)
                 for t in (cos_t, sin_t, decay_t, zeta_t, xi_t, chunk_decay))


def kernel(x, norm1_g, w_in, conv_w, conv_norm_g, ret_norm_g, w_out, norm2_g, w_up, w_down, final_norm_g):
    batch, seq, d_model = x.shape
    assert d_model == D_MODEL and w_in.shape == (D_MODEL, IN_COLS)
    assert seq % MIXER_TILE == 0 and MIXER_TILE % CHUNK == 0 and (batch * seq) % MLP_TILE == 0

    cos_t, sin_t, decay_t, zeta_t, xi_t, chunk_decay = _retention_tables(seq)
    row = lambda g: g.reshape(1, -1).astype(F32)
    n_units = (MIXER_TILE // CHUNK) * RET_HEADS

    tile_spec = pl.BlockSpec((None, MIXER_TILE, D_MODEL), lambda b, j: (b, j, 0))
    rope_spec = pl.BlockSpec((MIXER_TILE, HEAD_DIM), lambda b, j: (j, 0))
    seq_tiles = seq // MIXER_TILE
    n_steps = batch * seq_tiles
    assert D_MODEL % n_steps == 0 and D_FF % n_steps == 0
    wup_slab = pl.BlockSpec((D_MODEL // n_steps, D_FF), lambda b, j: (b * seq_tiles + j, 0))
    wdn_slab = pl.BlockSpec((D_FF // n_steps, D_MODEL), lambda b, j: (b * seq_tiles + j, 0))
    h, w_up_bf, w_down_bf = pl.pallas_call(
        _mixer_kernel,
        grid=(batch, seq_tiles),
        in_specs=[
            pl.BlockSpec(memory_space=pltpu.SMEM),
            tile_spec,
            _resident((1, D_MODEL)),
            pl.BlockSpec(memory_space=pl.ANY),
            _resident((CONV_K, CONV_WIDTH)),
            _resident((1, CONV_WIDTH)),
            _resident((1, RET_WIDTH)),
            pl.BlockSpec(memory_space=pl.ANY),
            rope_spec,
            rope_spec,
            _resident((RET_HEADS, CHUNK, CHUNK)),
            _resident((RET_HEADS, CHUNK, HEAD_DIM)),
            _resident((RET_HEADS, CHUNK, HEAD_DIM)),
            wup_slab,
            wdn_slab,
        ],
        out_specs=[tile_spec, wup_slab, wdn_slab],
        out_shape=[jax.ShapeDtypeStruct(x.shape, F32),
                   jax.ShapeDtypeStruct(w_up.shape, BF16),
                   jax.ShapeDtypeStruct(w_down.shape, BF16)],
        scratch_shapes=[
            pltpu.VMEM((D_MODEL, IN_COLS), BF16),
            pltpu.VMEM((D_MODEL, D_MODEL), BF16),
            pltpu.VMEM((CAST_SLOTS, CAST_ROWS, IN_COLS), F32),
            pltpu.VMEM((CAST_SLOTS, CAST_ROWS, D_MODEL), F32),
            pltpu.SemaphoreType.DMA((CAST_SLOTS,)),
            pltpu.SemaphoreType.DMA((CAST_SLOTS,)),
            pltpu.VMEM((RET_HEADS, HEAD_DIM, HEAD_DIM), F32),
            pltpu.VMEM((MIXER_TILE + SUBLANES, CONV_WIDTH), F32),
            pltpu.VMEM((MIXER_TILE + SUBLANES, CONV_WIDTH), F32),
            pltpu.VMEM((MIXER_TILE, D_MODEL), BF16),
            pltpu.VMEM((n_units, CHUNK, 2 * CHUNK), BF16),
            pltpu.VMEM((n_units, 2 * CHUNK, HEAD_DIM), BF16),
        ],
        compiler_params=pltpu.CompilerParams(
            dimension_semantics=("arbitrary", "arbitrary"),
            vmem_limit_bytes=VMEM_LIMIT_BYTES),
        name="mixer",
    )(chunk_decay, x, row(norm1_g), w_in, conv_w, row(conv_norm_g), row(ret_norm_g),
      w_out, cos_t, sin_t, decay_t, zeta_t, xi_t, w_up, w_down)

    tokens = batch * seq
    tok_spec = pl.BlockSpec((MLP_TILE, D_MODEL), lambda i: (i, 0))
    out = pl.pallas_call(
        _mlp_kernel,
        grid=(tokens // MLP_TILE,),
        in_specs=[
            tok_spec,
            _resident((1, D_MODEL)),
            _resident((D_MODEL, D_FF)),
            _resident((D_FF, D_MODEL)),
            _resident((1, D_MODEL)),
        ],
        out_specs=tok_spec,
        out_shape=jax.ShapeDtypeStruct((tokens, D_MODEL), F32),
        scratch_shapes=[pltpu.VMEM((MLP_TILE, D_FF), BF16)],
        compiler_params=pltpu.CompilerParams(
            dimension_semantics=("arbitrary",),
            vmem_limit_bytes=VMEM_LIMIT_BYTES),
        name="mlp",
    )(h.reshape(tokens, D_MODEL), row(norm2_g), w_up_bf, w_down_bf, row(final_norm_g))
    return out.reshape(batch, seq, D_MODEL)
```

```python
import functools

import jax
import jax.numpy as jnp
import numpy as np
from jax import lax
from jax.experimental import pallas as pl
from jax.experimental.pallas import tpu as pltpu

D_MODEL = 1024
CONV_WIDTH = 512
CONV_GROUPS = 8
CONV_GROUP_DIM = CONV_WIDTH // CONV_GROUPS
CONV_K = 3
RET_WIDTH = 512
RET_HEADS = 4
HEAD_DIM = RET_WIDTH // RET_HEADS
CHUNK = 128
ROPE_BASE = 10000.0
D_FF = 4 * D_MODEL
NORM_EPS = 1e-6
IN_COLS = 3 * CONV_WIDTH + 4 * RET_WIDTH

LANES = 128
SUBLANES = 8
VMEM_LIMIT_BYTES = 56 * 1024 * 1024

MIXER_TILE = 1024
MLP_TILE = 1024
MLP_DOWN_ROWS = 256
FF_CHUNK = 1024
WARM_K = 256
CAST_ROWS = 32
CAST_SLOTS = 8

BF16 = jnp.bfloat16
F32 = jnp.float32


def _dot(a, b):
    return jnp.dot(a, b, preferred_element_type=F32)


def _rms_scale(x):
    return lax.rsqrt(jnp.mean(x * x, axis=-1, keepdims=True) + NORM_EPS)


def _bf16_cast_ring(src_hbm, dst_ref, stage_ref, sem_ref):
    slots, rows = stage_ref.shape[0], stage_ref.shape[1]
    n = src_hbm.shape[0] // rows

    def copy(k):
        slot = k % slots
        return pltpu.make_async_copy(src_hbm.at[pl.ds(k * rows, rows), :], stage_ref.at[slot], sem_ref.at[slot])

    def prime():
        for k in range(min(slots, n)):
            copy(k).start()

    def drain():
        for k in range(n):
            copy(k).wait()
            dst_ref[k * rows:(k + 1) * rows, :] = stage_ref[k % slots].astype(BF16)
            if k + slots < n:
                copy(k + slots).start()

    return prime, drain


def _mixer_kernel(cd_ref, x_ref, g1_ref, win_hbm, convw_ref, cng_ref, rng_ref, wout_hbm,
                  cos_ref, sin_ref, decay_ref, zeta_ref, xi_ref, wup_ref, wdn_ref,
                  h_ref, wup_bf_ref, wdn_bf_ref,
                  win_ref, wout_ref, win_stage_ref, wout_stage_ref, win_sem_ref, wout_sem_ref,
                  state_ref, pd1_ref, pd2_ref, mix_ref, lhs_ref, rhs_ref):
    tile = x_ref.shape[0]
    n_chunks = tile // CHUNK
    units = [(c, hd) for c in range(n_chunks) for hd in range(RET_HEADS)]

    @pl.when((pl.program_id(0) == 0) & (pl.program_id(1) == 0))
    def _():
        prime_in, drain_in = _bf16_cast_ring(win_hbm, win_ref, win_stage_ref, win_sem_ref)
        prime_out, drain_out = _bf16_cast_ring(wout_hbm, wout_ref, wout_stage_ref, wout_sem_ref)
        prime_in()
        prime_out()
        drain_in()
        drain_out()

    wup_bf_ref[...] = wup_ref[...].astype(BF16)
    wdn_bf_ref[...] = wdn_ref[...].astype(BF16)

    @pl.when(pl.program_id(1) == 0)
    def _():
        state_ref[...] = jnp.zeros_like(state_ref)
        zeros = jnp.zeros((SUBLANES, CONV_WIDTH), F32)
        pd1_ref[0:SUBLANES, :] = zeros
        pd2_ref[0:SUBLANES, :] = zeros
        pd1_ref[tile:tile + SUBLANES, :] = zeros
        pd2_ref[tile:tile + SUBLANES, :] = zeros

    warm = jnp.dot(jnp.zeros((tile // 2, WARM_K), BF16), win_ref[0:WARM_K, 0:2 * LANES],
                   preferred_element_type=F32)

    x = x_ref[...]
    u = (x * _rms_scale(x) * g1_ref[...]).astype(BF16)

    def in_proj(col0, width):
        return jnp.dot(u, win_ref[:, col0:col0 + width], preferred_element_type=F32)

    c0 = 3 * CONV_WIDTH

    zq = in_proj(c0, RET_WIDTH)
    zk = in_proj(c0 + RET_WIDTH, RET_WIDTH)
    cb = in_proj(0, CONV_WIDTH)
    cc = in_proj(CONV_WIDTH, CONV_WIDTH)

    cos = cos_ref[...]
    sin = sin_ref[...]
    q_rot, kt_rot = [], []
    for hd in range(RET_HEADS):
        hs = slice(hd * HEAD_DIM, (hd + 1) * HEAD_DIM)
        qh = zq[:, hs]
        kh = zk[:, hs]
        q_rot.append(qh * cos + pltpu.roll(qh, HEAD_DIM // 2, axis=1) * sin)
        kt_rot.append((kh * cos + pltpu.roll(kh, HEAD_DIM // 2, axis=1) * sin).T)

    for n, (c, hd) in enumerate(units):
        rows = slice(c * CHUNK, (c + 1) * CHUNK)
        q = q_rot[hd][rows]
        kt = kt_rot[hd][:, rows]
        s = jnp.dot(q.astype(BF16), kt.astype(BF16), preferred_element_type=F32) * decay_ref[hd]
        lhs_ref[n, :, 0:CHUNK] = s.astype(BF16)
        lhs_ref[n, :, CHUNK:2 * CHUNK] = (q * xi_ref[hd]).astype(BF16)

    ch = in_proj(2 * CONV_WIDTH, CONV_WIDTH)
    zv = in_proj(c0 + 2 * RET_WIDTH, RET_WIDTH)
    v_bf = zv.astype(BF16)

    states = [state_ref[hd] for hd in range(RET_HEADS)]
    for n, (c, hd) in enumerate(units):
        rows = slice(c * CHUNK, (c + 1) * CHUNK)
        v = v_bf[rows, hd * HEAD_DIM:(hd + 1) * HEAD_DIM]
        kzt = (kt_rot[hd][:, rows] * zeta_ref[hd]).astype(BF16)
        kv = jnp.dot(kzt, v, preferred_element_type=F32)
        rhs_ref[n, 0:CHUNK, :] = v
        rhs_ref[n, CHUNK:2 * CHUNK, :] = states[hd].astype(BF16)
        states[hd] = cd_ref[hd] * states[hd] + kv
    for hd in range(RET_HEADS):
        state_ref[hd] = states[hd]

    zg = in_proj(c0 + 3 * RET_WIDTH, RET_WIDTH)

    p = cc * ch
    pd1_ref[1:1 + tile, :] = p
    pd2_ref[2:2 + tile, :] = p
    p1 = pd1_ref[0:tile, :]
    p2 = pd2_ref[0:tile, :]
    y = cb * (p2 * convw_ref[0:1, :] + p1 * convw_ref[1:2, :] + p * convw_ref[2:3, :])
    pd1_ref[0:SUBLANES, :] = pd1_ref[tile:tile + SUBLANES, :]
    pd2_ref[0:SUBLANES, :] = pd2_ref[tile:tile + SUBLANES, :]

    lane = lax.broadcasted_iota(jnp.int32, (tile, LANES), 1)
    low = lane < CONV_GROUP_DIM
    for blk in range(CONV_WIDTH // LANES):
        sl = slice(blk * LANES, (blk + 1) * LANES)
        yb = y[:, sl]
        y2 = yb * yb
        ss_lo = jnp.sum(jnp.where(low, y2, 0.0), axis=-1, keepdims=True)
        ss_hi = jnp.sum(jnp.where(low, 0.0, y2), axis=-1, keepdims=True)
        inv = lax.rsqrt(jnp.where(low, ss_lo, ss_hi) * (1.0 / CONV_GROUP_DIM) + NORM_EPS)
        mix_ref[:, sl] = (yb * inv * cng_ref[:, sl]).astype(BF16)

    outs = [jnp.dot(lhs_ref[n], rhs_ref[n], preferred_element_type=F32) for n in range(len(units))]
    for hd in range(RET_HEADS):
        hs = slice(hd * HEAD_DIM, (hd + 1) * HEAD_DIM)
        o = jnp.concatenate([outs[c * RET_HEADS + hd] for c in range(n_chunks)], axis=0)
        gate = zg[:, hs]
        gate = gate * (1.0 / (1.0 + jnp.exp(-gate)))
        yr = o * _rms_scale(o) * rng_ref[:, hs] * gate
        mix_ref[:, CONV_WIDTH + hd * HEAD_DIM:CONV_WIDTH + (hd + 1) * HEAD_DIM] = yr.astype(BF16)

    h_ref[...] = x_ref[...] + jnp.dot(mix_ref[...], wout_ref[...], preferred_element_type=F32)
    h_ref[0:SUBLANES, 0:LANES] += warm[0:SUBLANES, 0:LANES]


def _mlp_kernel(h_ref, g2_ref, wup_ref, wdn_ref, gf_ref, o_ref, hid_ref):
    warm = jnp.dot(jnp.zeros((h_ref.shape[0] // 2, WARM_K), BF16), wup_ref[0:WARM_K, 0:2 * LANES],
                   preferred_element_type=F32)
    h = h_ref[...]
    u = (h * _rms_scale(h) * g2_ref[...]).astype(BF16)
    for c in range(D_FF // FF_CHUNK):
        cols = slice(c * FF_CHUNK, (c + 1) * FF_CHUNK)
        a = jnp.maximum(_dot(u, wup_ref[:, cols]), 0.0)
        hid_ref[:, cols] = (a * a).astype(BF16)
    for r in range(h_ref.shape[0] // MLP_DOWN_ROWS):
        rows = slice(r * MLP_DOWN_ROWS, (r + 1) * MLP_DOWN_ROWS)
        y = h_ref[rows, :] + _dot(hid_ref[rows, :], wdn_ref[...])
        o_ref[rows, :] = y * _rms_scale(y) * gf_ref[...]
    o_ref[0:SUBLANES, 0:LANES] += warm[0:SUBLANES, 0:LANES]


def _resident(shape):
    nd = len(shape)
    return pl.BlockSpec(shape, lambda *_: (0,) * nd, pipeline_mode=pl.Buffered(1))


@functools.lru_cache(maxsize=None)
def _retention_tables(seq):
    half = HEAD_DIM // 2
    inv_freq = 1.0 / (ROPE_BASE ** (np.arange(half, dtype=np.float64) / half))
    ang = np.arange(seq, dtype=np.float64)[:, None] * inv_freq[None, :]
    cos = np.cos(ang)
    sin = np.sin(ang)
    cos_t = np.concatenate([cos, cos], axis=-1)
    sin_t = np.concatenate([-sin, sin], axis=-1)

    log_gamma = np.log(1.0 - 2.0 ** (-5.0 - np.arange(RET_HEADS, dtype=np.float64)))
    idx = np.arange(CHUNK, dtype=np.float64)
    diff = idx[:, None] - idx[None, :]
    intra = np.where(diff[None] >= 0, np.exp(log_gamma[:, None, None] * np.maximum(diff, 0.0)[None]), 0.0)
    zeta = np.exp(log_gamma[:, None] * (CHUNK - 1 - idx)[None])
    xi = np.exp(log_gamma[:, None] * (idx + 1.0)[None])
    chunk_decay = np.exp(log_gamma * CHUNK)
    k_scale = HEAD_DIM ** -0.5
    decay_t = intra * k_scale
    zeta_t = np.broadcast_to((zeta * k_scale)[:, None, :], (RET_HEADS, HEAD_DIM, CHUNK))
    xi_t = np.broadcast_to(xi[:, :, None], (RET_HEADS, CHUNK, HEAD_DIM))
    return tuple(np.ascontiguousarray(t, dtype=np.float32)
                 for t in (cos_t, sin_t, decay_t, zeta_t, xi_t, chunk_decay))


def kernel(x, norm1_g, w_in, conv_w, conv_norm_g, ret_norm_g, w_out, norm2_g, w_up, w_down, final_norm_g):
    batch, seq, d_model = x.shape
    assert d_model == D_MODEL and w_in.shape == (D_MODEL, IN_COLS)
    assert seq % MIXER_TILE == 0 and MIXER_TILE % CHUNK == 0 and (batch * seq) % MLP_TILE == 0

    cos_t, sin_t, decay_t, zeta_t, xi_t, chunk_decay = _retention_tables(seq)
    row = lambda g: g.reshape(1, -1).astype(F32)
    n_units = (MIXER_TILE // CHUNK) * RET_HEADS

    tile_spec = pl.BlockSpec((None, MIXER_TILE, D_MODEL), lambda b, j: (b, j, 0))
    rope_spec = pl.BlockSpec((MIXER_TILE, HEAD_DIM), lambda b, j: (j, 0))
    seq_tiles = seq // MIXER_TILE
    n_steps = batch * seq_tiles
    assert D_MODEL % n_steps == 0 and D_FF % n_steps == 0
    wup_slab = pl.BlockSpec((D_MODEL // n_steps, D_FF), lambda b, j: (b * seq_tiles + j, 0))
    wdn_slab = pl.BlockSpec((D_FF // n_steps, D_MODEL), lambda b, j: (b * seq_tiles + j, 0))
    h, w_up_bf, w_down_bf = pl.pallas_call(
        _mixer_kernel,
        grid=(batch, seq_tiles),
        in_specs=[
            pl.BlockSpec(memory_space=pltpu.SMEM),
            tile_spec,
            _resident((1, D_MODEL)),
            pl.BlockSpec(memory_space=pl.ANY),
            _resident((CONV_K, CONV_WIDTH)),
            _resident((1, CONV_WIDTH)),
            _resident((1, RET_WIDTH)),
            pl.BlockSpec(memory_space=pl.ANY),
            rope_spec,
            rope_spec,
            _resident((RET_HEADS, CHUNK, CHUNK)),
            _resident((RET_HEADS, CHUNK, HEAD_DIM)),
            _resident((RET_HEADS, CHUNK, HEAD_DIM)),
            wup_slab,
            wdn_slab,
        ],
        out_specs=[tile_spec, wup_slab, wdn_slab],
        out_shape=[jax.ShapeDtypeStruct(x.shape, F32),
                   jax.ShapeDtypeStruct(w_up.shape, BF16),
                   jax.ShapeDtypeStruct(w_down.shape, BF16)],
        scratch_shapes=[
            pltpu.VMEM((D_MODEL, IN_COLS), BF16),
            pltpu.VMEM((D_MODEL, D_MODEL), BF16),
            pltpu.VMEM((CAST_SLOTS, CAST_ROWS, IN_COLS), F32),
            pltpu.VMEM((CAST_SLOTS, CAST_ROWS, D_MODEL), F32),
            pltpu.SemaphoreType.DMA((CAST_SLOTS,)),
            pltpu.SemaphoreType.DMA((CAST_SLOTS,)),
            pltpu.VMEM((RET_HEADS, HEAD_DIM, HEAD_DIM), F32),
            pltpu.VMEM((MIXER_TILE + SUBLANES, CONV_WIDTH), F32),
            pltpu.VMEM((MIXER_TILE + SUBLANES, CONV_WIDTH), F32),
            pltpu.VMEM((MIXER_TILE, D_MODEL), BF16),
            pltpu.VMEM((n_units, CHUNK, 2 * CHUNK), BF16),
            pltpu.VMEM((n_units, 2 * CHUNK, HEAD_DIM), BF16),
        ],
        compiler_params=pltpu.CompilerParams(
            dimension_semantics=("arbitrary", "arbitrary"),
            vmem_limit_bytes=VMEM_LIMIT_BYTES),
        name="mixer",
    )(chunk_decay, x, row(norm1_g), w_in, conv_w, row(conv_norm_g), row(ret_norm_g),
      w_out, cos_t, sin_t, decay_t, zeta_t, xi_t, w_up, w_down)

    tokens = batch * seq
    tok_spec = pl.BlockSpec((MLP_TILE, D_MODEL), lambda i: (i, 0))
    out = pl.pallas_call(
        _mlp_kernel,
        grid=(tokens // MLP_TILE,),
        in_specs=[
            tok_spec,
            _resident((1, D_MODEL)),
            _resident((D_MODEL, D_FF)),
            _resident((D_FF, D_MODEL)),
            _resident((1, D_MODEL)),
        ],
        out_specs=tok_spec,
        out_shape=jax.ShapeDtypeStruct((tokens, D_MODEL), F32),
        scratch_shapes=[pltpu.VMEM((MLP_TILE, D_FF), BF16)],
        compiler_params=pltpu.CompilerParams(
            dimension_semantics=("arbitrary",),
            vmem_limit_bytes=VMEM_LIMIT_BYTES),
        name="mlp",
    )(h.reshape(tokens, D_MODEL), row(norm2_g), w_up_bf, w_down_bf, row(final_norm_g))
    return out.reshape(batch, seq, D_MODEL)
```

```python
import functools

import jax
import jax.numpy as jnp
import numpy as np
from jax import lax
from jax.experimental import pallas as pl
from jax.experimental.pallas import tpu as pltpu

D_MODEL = 1024
CONV_WIDTH = 512
CONV_GROUPS = 8
CONV_GROUP_DIM = CONV_WIDTH // CONV_GROUPS
CONV_K = 3
RET_WIDTH = 512
RET_HEADS = 4
HEAD_DIM = RET_WIDTH // RET_HEADS
CHUNK = 128
ROPE_BASE = 10000.0
D_FF = 4 * D_MODEL
NORM_EPS = 1e-6
IN_COLS = 3 * CONV_WIDTH + 4 * RET_WIDTH

LANES = 128
SUBLANES = 8
VMEM_LIMIT_BYTES = 56 * 1024 * 1024

MIXER_TILE = 1024
MLP_TILE = 1024
MLP_DOWN_ROWS = 256
FF_CHUNK = 1024
WARM_K = 256
CAST_ROWS = 32
CAST_SLOTS = 8

BF16 = jnp.bfloat16
F32 = jnp.float32


def _dot(a, b):
    return jnp.dot(a, b, preferred_element_type=F32)


def _rms_scale(x):
    return lax.rsqrt(jnp.mean(x * x, axis=-1, keepdims=True) + NORM_EPS)


def _bf16_cast_ring(src_hbm, dst_ref, stage_ref, sem_ref):
    slots, rows = stage_ref.shape[0], stage_ref.shape[1]
    n = src_hbm.shape[0] // rows

    def copy(k):
        slot = k % slots
        return pltpu.make_async_copy(src_hbm.at[pl.ds(k * rows, rows), :], stage_ref.at[slot], sem_ref.at[slot])

    def prime():
        for k in range(min(slots, n)):
            copy(k).start()

    def drain():
        for k in range(n):
            copy(k).wait()
            dst_ref[k * rows:(k + 1) * rows, :] = stage_ref[k % slots].astype(BF16)
            if k + slots < n:
                copy(k + slots).start()

    return prime, drain


def _mixer_kernel(cd_ref, x_ref, g1_ref, win_hbm, convw_ref, cng_ref, rng_ref, wout_hbm,
                  cos_ref, sin_ref, decay_ref, zeta_ref, xi_ref, wup_ref, wdn_ref,
                  h_ref, wup_bf_ref, wdn_bf_ref,
                  win_ref, wout_ref, win_stage_ref, wout_stage_ref, win_sem_ref, wout_sem_ref,
                  state_ref, pd1_ref, pd2_ref, mix_ref, lhs_ref, rhs_ref):
    tile = x_ref.shape[0]
    n_chunks = tile // CHUNK
    units = [(c, hd) for c in range(n_chunks) for hd in range(RET_HEADS)]

    @pl.when((pl.program_id(0) == 0) & (pl.program_id(1) == 0))
    def _():
        prime_in, drain_in = _bf16_cast_ring(win_hbm, win_ref, win_stage_ref, win_sem_ref)
        prime_out, drain_out = _bf16_cast_ring(wout_hbm, wout_ref, wout_stage_ref, wout_sem_ref)
        prime_in()
        prime_out()
        drain_in()
        drain_out()

    wup_bf_ref[...] = wup_ref[...].astype(BF16)
    wdn_bf_ref[...] = wdn_ref[...].astype(BF16)

    @pl.when(pl.program_id(1) == 0)
    def _():
        state_ref[...] = jnp.zeros_like(state_ref)
        zeros = jnp.zeros((SUBLANES, CONV_WIDTH), F32)
        pd1_ref[0:SUBLANES, :] = zeros
        pd2_ref[0:SUBLANES, :] = zeros
        pd1_ref[tile:tile + SUBLANES, :] = zeros
        pd2_ref[tile:tile + SUBLANES, :] = zeros

    x = x_ref[...]
    xg = x * g1_ref[...]
    cb_raw = jnp.dot(xg.astype(BF16), win_ref[:, 0:CONV_WIDTH], preferred_element_type=F32)
    x_scale = _rms_scale(x)
    u = (xg * x_scale).astype(BF16)

    def in_proj(col0, width):
        return jnp.dot(u, win_ref[:, col0:col0 + width], preferred_element_type=F32)

    c0 = 3 * CONV_WIDTH

    zq = in_proj(c0, RET_WIDTH)
    zk = in_proj(c0 + RET_WIDTH, RET_WIDTH)
    cc = in_proj(CONV_WIDTH, CONV_WIDTH)

    cos = cos_ref[...]
    sin = sin_ref[...]
    q_rot, kt_rot = [], []
    for hd in range(RET_HEADS):
        hs = slice(hd * HEAD_DIM, (hd + 1) * HEAD_DIM)
        qh = zq[:, hs]
        kh = zk[:, hs]
        q_rot.append(qh * cos + pltpu.roll(qh, HEAD_DIM // 2, axis=1) * sin)
        kt_rot.append((kh * cos + pltpu.roll(kh, HEAD_DIM // 2, axis=1) * sin).T)

    for n, (c, hd) in enumerate(units):
        rows = slice(c * CHUNK, (c + 1) * CHUNK)
        q = q_rot[hd][rows]
        kt = kt_rot[hd][:, rows]
        s = jnp.dot(q.astype(BF16), kt.astype(BF16), preferred_element_type=F32) * decay_ref[hd]
        lhs_ref[n, :, 0:CHUNK] = s.astype(BF16)
        lhs_ref[n, :, CHUNK:2 * CHUNK] = (q * xi_ref[hd]).astype(BF16)

    ch = in_proj(2 * CONV_WIDTH, CONV_WIDTH)
    zv = in_proj(c0 + 2 * RET_WIDTH, RET_WIDTH)
    v_bf = zv.astype(BF16)

    states = [state_ref[hd] for hd in range(RET_HEADS)]
    for n, (c, hd) in enumerate(units):
        rows = slice(c * CHUNK, (c + 1) * CHUNK)
        v = v_bf[rows, hd * HEAD_DIM:(hd + 1) * HEAD_DIM]
        kzt = (kt_rot[hd][:, rows] * zeta_ref[hd]).astype(BF16)
        kv = jnp.dot(kzt, v, preferred_element_type=F32)
        rhs_ref[n, 0:CHUNK, :] = v
        rhs_ref[n, CHUNK:2 * CHUNK, :] = states[hd].astype(BF16)
        states[hd] = cd_ref[hd] * states[hd] + kv
    for hd in range(RET_HEADS):
        state_ref[hd] = states[hd]

    zg = in_proj(c0 + 3 * RET_WIDTH, RET_WIDTH)

    p = cc * ch
    pd1_ref[1:1 + tile, :] = p
    pd2_ref[2:2 + tile, :] = p
    p1 = pd1_ref[0:tile, :]
    p2 = pd2_ref[0:tile, :]
    y = (cb_raw * x_scale) * (p2 * convw_ref[0:1, :] + p1 * convw_ref[1:2, :] + p * convw_ref[2:3, :])
    pd1_ref[0:SUBLANES, :] = pd1_ref[tile:tile + SUBLANES, :]
    pd2_ref[0:SUBLANES, :] = pd2_ref[tile:tile + SUBLANES, :]

    lane = lax.broadcasted_iota(jnp.int32, (tile, LANES), 1)
    low = lane < CONV_GROUP_DIM
    for blk in range(CONV_WIDTH // LANES):
        sl = slice(blk * LANES, (blk + 1) * LANES)
        yb = y[:, sl]
        y2 = yb * yb
        ss_lo = jnp.sum(jnp.where(low, y2, 0.0), axis=-1, keepdims=True)
        ss_hi = jnp.sum(jnp.where(low, 0.0, y2), axis=-1, keepdims=True)
        inv = lax.rsqrt(jnp.where(low, ss_lo, ss_hi) * (1.0 / CONV_GROUP_DIM) + NORM_EPS)
        mix_ref[:, sl] = (yb * inv * cng_ref[:, sl]).astype(BF16)

    outs = [jnp.dot(lhs_ref[n], rhs_ref[n], preferred_element_type=F32) for n in range(len(units))]
    for hd in range(RET_HEADS):
        hs = slice(hd * HEAD_DIM, (hd + 1) * HEAD_DIM)
        o = jnp.concatenate([outs[c * RET_HEADS + hd] for c in range(n_chunks)], axis=0)
        gate = zg[:, hs]
        gate = gate * (1.0 / (1.0 + jnp.exp(-gate)))
        yr = o * _rms_scale(o) * rng_ref[:, hs] * gate
        mix_ref[:, CONV_WIDTH + hd * HEAD_DIM:CONV_WIDTH + (hd + 1) * HEAD_DIM] = yr.astype(BF16)

    h_ref[...] = x_ref[...] + jnp.dot(mix_ref[...], wout_ref[...], preferred_element_type=F32)


def _mlp_kernel(h_ref, g2_ref, wup_ref, wdn_ref, gf_ref, o_ref, hid_ref):
    warm = jnp.dot(jnp.zeros((h_ref.shape[0], WARM_K), BF16), wup_ref[0:WARM_K, 0:2 * LANES],
                   preferred_element_type=F32)
    h = h_ref[...]
    u = (h * _rms_scale(h) * g2_ref[...]).astype(BF16)
    for c in range(D_FF // FF_CHUNK):
        cols = slice(c * FF_CHUNK, (c + 1) * FF_CHUNK)
        a = jnp.maximum(_dot(u, wup_ref[:, cols]), 0.0)
        hid_ref[:, cols] = (a * a).astype(BF16)
    for r in range(h_ref.shape[0] // MLP_DOWN_ROWS):
        rows = slice(r * MLP_DOWN_ROWS, (r + 1) * MLP_DOWN_ROWS)
        y = h_ref[rows, :] + _dot(hid_ref[rows, :], wdn_ref[...])
        o_ref[rows, :] = y * _rms_scale(y) * gf_ref[...]
    o_ref[0:SUBLANES, 0:LANES] += warm[0:SUBLANES, 0:LANES]


def _resident(shape):
    nd = len(shape)
    return pl.BlockSpec(shape, lambda *_: (0,) * nd, pipeline_mode=pl.Buffered(1))


@functools.lru_cache(maxsize=None)
def _retention_tables(seq):
    half = HEAD_DIM // 2
    inv_freq = 1.0 / (ROPE_BASE ** (np.arange(half, dtype=np.float64) / half))
    ang = np.arange(seq, dtype=np.float64)[:, None] * inv_freq[None, :]
    cos = np.cos(ang)
    sin = np.sin(ang)
    cos_t = np.concatenate([cos, cos], axis=-1)
    sin_t = np.concatenate([-sin, sin], axis=-1)

    log_gamma = np.log(1.0 - 2.0 ** (-5.0 - np.arange(RET_HEADS, dtype=np.float64)))
    idx = np.arange(CHUNK, dtype=np.float64)
    diff = idx[:, None] - idx[None, :]
    intra = np.where(diff[None] >= 0, np.exp(log_gamma[:, None, None] * np.maximum(diff, 0.0)[None]), 0.0)
    zeta = np.exp(log_gamma[:, None] * (CHUNK - 1 - idx)[None])
    xi = np.exp(log_gamma[:, None] * (idx + 1.0)[None])
    chunk_decay = np.exp(log_gamma * CHUNK)
    k_scale = HEAD_DIM ** -0.5
    decay_t = intra * k_scale
    zeta_t = np.broadcast_to((zeta * k_scale)[:, None, :], (RET_HEADS, HEAD_DIM, CHUNK))
    xi_t = np.broadcast_to(xi[:, :, None], (RET_HEADS, CHUNK, HEAD_DIM))
    return tuple(np.ascontiguousarray(t, dtype=np.float32)
                 for t in (cos_t, sin_t, decay_t, zeta_t, xi_t, chunk_decay))


def kernel(x, norm1_g, w_in, conv_w, conv_norm_g, ret_norm_g, w_out, norm2_g, w_up, w_down, final_norm_g):
    batch, seq, d_model = x.shape
    assert d_model == D_MODEL and w_in.shape == (D_MODEL, IN_COLS)
    assert seq % MIXER_TILE == 0 and MIXER_TILE % CHUNK == 0 and (batch * seq) % MLP_TILE == 0

    cos_t, sin_t, decay_t, zeta_t, xi_t, chunk_decay = _retention_tables(seq)
    row = lambda g: g.reshape(1, -1).astype(F32)
    n_units = (MIXER_TILE // CHUNK) * RET_HEADS

    tile_spec = pl.BlockSpec((None, MIXER_TILE, D_MODEL), lambda b, j: (b, j, 0))
    rope_spec = pl.BlockSpec((MIXER_TILE, HEAD_DIM), lambda b, j: (j, 0))
    seq_tiles = seq // MIXER_TILE
    n_steps = batch * seq_tiles
    assert D_MODEL % n_steps == 0 and D_FF % n_steps == 0
    wup_slab = pl.BlockSpec((D_MODEL // n_steps, D_FF), lambda b, j: (b * seq_tiles + j, 0))
    wdn_slab = pl.BlockSpec((D_FF // n_steps, D_MODEL), lambda b, j: (b * seq_tiles + j, 0))
    h, w_up_bf, w_down_bf = pl.pallas_call(
        _mixer_kernel,
        grid=(batch, seq_tiles),
        in_specs=[
            pl.BlockSpec(memory_space=pltpu.SMEM),
            tile_spec,
            _resident((1, D_MODEL)),
            pl.BlockSpec(memory_space=pl.ANY),
            _resident((CONV_K, CONV_WIDTH)),
            _resident((1, CONV_WIDTH)),
            _resident((1, RET_WIDTH)),
            pl.BlockSpec(memory_space=pl.ANY),
            rope_spec,
            rope_spec,
            _resident((RET_HEADS, CHUNK, CHUNK)),
            _resident((RET_HEADS, CHUNK, HEAD_DIM)),
            _resident((RET_HEADS, CHUNK, HEAD_DIM)),
            wup_slab,
            wdn_slab,
        ],
        out_specs=[tile_spec, wup_slab, wdn_slab],
        out_shape=[jax.ShapeDtypeStruct(x.shape, F32),
                   jax.ShapeDtypeStruct(w_up.shape, BF16),
                   jax.ShapeDtypeStruct(w_down.shape, BF16)],
        scratch_shapes=[
            pltpu.VMEM((D_MODEL, IN_COLS), BF16),
            pltpu.VMEM((D_MODEL, D_MODEL), BF16),
            pltpu.VMEM((CAST_SLOTS, CAST_ROWS, IN_COLS), F32),
            pltpu.VMEM((CAST_SLOTS, CAST_ROWS, D_MODEL), F32),
            pltpu.SemaphoreType.DMA((CAST_SLOTS,)),
            pltpu.SemaphoreType.DMA((CAST_SLOTS,)),
            pltpu.VMEM((RET_HEADS, HEAD_DIM, HEAD_DIM), F32),
            pltpu.VMEM((MIXER_TILE + SUBLANES, CONV_WIDTH), F32),
            pltpu.VMEM((MIXER_TILE + SUBLANES, CONV_WIDTH), F32),
            pltpu.VMEM((MIXER_TILE, D_MODEL), BF16),
            pltpu.VMEM((n_units, CHUNK, 2 * CHUNK), BF16),
            pltpu.VMEM((n_units, 2 * CHUNK, HEAD_DIM), BF16),
        ],
        compiler_params=pltpu.CompilerParams(
            dimension_semantics=("arbitrary", "arbitrary"),
            vmem_limit_bytes=VMEM_LIMIT_BYTES),
        name="mixer",
    )(chunk_decay, x, row(norm1_g), w_in, conv_w, row(conv_norm_g), row(ret_norm_g),
      w_out, cos_t, sin_t, decay_t, zeta_t, xi_t, w_up, w_down)

    tokens = batch * seq
    tok_spec = pl.BlockSpec((MLP_TILE, D_MODEL), lambda i: (i, 0))
    out = pl.pallas_call(
        _mlp_kernel,
        grid=(tokens // MLP_TILE,),
        in_specs=[
            tok_spec,
            _resident((1, D_MODEL)),
            _resident((D_MODEL, D_FF)),
            _resident((D_FF, D_MODEL)),
            _resident((1, D_MODEL)),
        ],
        out_specs=tok_spec,
        out_shape=jax.ShapeDtypeStruct((tokens, D_MODEL), F32),
        scratch_shapes=[pltpu.VMEM((MLP_TILE, D_FF), BF16)],
        compiler_params=pltpu.CompilerParams(
            dimension_semantics=("arbitrary",),
            vmem_limit_bytes=VMEM_LIMIT_BYTES),
        name="mlp",
    )(h.reshape(tokens, D_MODEL), row(norm2_g), w_up_bf, w_down_bf, row(final_norm_g))
    return out.reshape(batch, seq, D_MODEL)
```

```python
import functools

import jax
import jax.numpy as jnp
import numpy as np
from jax import lax
from jax.experimental import pallas as pl
from jax.experimental.pallas import tpu as pltpu

D_MODEL = 1024
CONV_WIDTH = 512
CONV_GROUPS = 8
CONV_GROUP_DIM = CONV_WIDTH // CONV_GROUPS
CONV_K = 3
RET_WIDTH = 512
RET_HEADS = 4
HEAD_DIM = RET_WIDTH // RET_HEADS
CHUNK = 128
ROPE_BASE = 10000.0
D_FF = 4 * D_MODEL
NORM_EPS = 1e-6
IN_COLS = 3 * CONV_WIDTH + 4 * RET_WIDTH

LANES = 128
SUBLANES = 8
VMEM_LIMIT_BYTES = 56 * 1024 * 1024

MIXER_TILE = 1024
MLP_TILE = 1024
MLP_DOWN_ROWS = 256
FF_CHUNK = 1024
WARM_K = 256
CAST_ROWS = 32
CAST_SLOTS = 8

BF16 = jnp.bfloat16
F32 = jnp.float32


def _dot(a, b):
    return jnp.dot(a, b, preferred_element_type=F32)


def _rms_scale(x):
    return lax.rsqrt(jnp.mean(x * x, axis=-1, keepdims=True) + NORM_EPS)


def _bf16_cast_ring(src_hbm, dst_ref, stage_ref, sem_ref):
    slots, rows = stage_ref.shape[0], stage_ref.shape[1]
    n = src_hbm.shape[0] // rows

    def copy(k):
        slot = k % slots
        return pltpu.make_async_copy(src_hbm.at[pl.ds(k * rows, rows), :], stage_ref.at[slot], sem_ref.at[slot])

    def prime():
        for k in range(min(slots, n)):
            copy(k).start()

    def drain():
        for k in range(n):
            copy(k).wait()
            dst_ref[k * rows:(k + 1) * rows, :] = stage_ref[k % slots].astype(BF16)
            if k + slots < n:
                copy(k + slots).start()

    return prime, drain


def _mixer_kernel(cd_ref, x_ref, g1_ref, win_hbm, convw_ref, cng_ref, rng_ref,
                  cos_ref, sin_ref, decay_ref, zeta_ref, xi_ref, wup_ref, wdn_ref, wout_ref,
                  mix_ref, wup_bf_ref, wdn_bf_ref, wout_bf_ref,
                  win_ref, win_stage_ref, win_sem_ref,
                  state_ref, pd1_ref, pd2_ref, lhs_ref, rhs_ref):
    tile = x_ref.shape[0]
    n_chunks = tile // CHUNK
    units = [(c, hd) for c in range(n_chunks) for hd in range(RET_HEADS)]

    @pl.when((pl.program_id(0) == 0) & (pl.program_id(1) == 0))
    def _():
        prime_in, drain_in = _bf16_cast_ring(win_hbm, win_ref, win_stage_ref, win_sem_ref)
        prime_in()
        drain_in()

    wup_bf_ref[...] = wup_ref[...].astype(BF16)
    wdn_bf_ref[...] = wdn_ref[...].astype(BF16)
    wout_bf_ref[...] = wout_ref[...].astype(BF16)

    @pl.when(pl.program_id(1) == 0)
    def _():
        state_ref[...] = jnp.zeros_like(state_ref)
        zeros = jnp.zeros((SUBLANES, CONV_WIDTH), F32)
        pd1_ref[0:SUBLANES, :] = zeros
        pd2_ref[0:SUBLANES, :] = zeros
        pd1_ref[tile:tile + SUBLANES, :] = zeros
        pd2_ref[tile:tile + SUBLANES, :] = zeros

    warm = jnp.dot(jnp.zeros((tile // 2, WARM_K), BF16), win_ref[0:WARM_K, 0:2 * LANES],
                   preferred_element_type=F32)

    x = x_ref[...]
    u = (x * _rms_scale(x) * g1_ref[...]).astype(BF16)

    def in_proj(col0, width):
        return jnp.dot(u, win_ref[:, col0:col0 + width], preferred_element_type=F32)

    c0 = 3 * CONV_WIDTH

    zq = in_proj(c0, RET_WIDTH)
    zk = in_proj(c0 + RET_WIDTH, RET_WIDTH)
    cb = in_proj(0, CONV_WIDTH)
    cc = in_proj(CONV_WIDTH, CONV_WIDTH)

    cos = cos_ref[...]
    sin = sin_ref[...]
    q_rot, kt_rot = [], []
    for hd in range(RET_HEADS):
        hs = slice(hd * HEAD_DIM, (hd + 1) * HEAD_DIM)
        qh = zq[:, hs]
        kh = zk[:, hs]
        q_rot.append(qh * cos + pltpu.roll(qh, HEAD_DIM // 2, axis=1) * sin)
        kt_rot.append((kh * cos + pltpu.roll(kh, HEAD_DIM // 2, axis=1) * sin).T)

    for n, (c, hd) in enumerate(units):
        rows = slice(c * CHUNK, (c + 1) * CHUNK)
        q = q_rot[hd][rows]
        kt = kt_rot[hd][:, rows]
        s = jnp.dot(q.astype(BF16), kt.astype(BF16), preferred_element_type=F32) * decay_ref[hd]
        lhs_ref[n, :, 0:CHUNK] = s.astype(BF16)
        lhs_ref[n, :, CHUNK:2 * CHUNK] = (q * xi_ref[hd]).astype(BF16)

    ch = in_proj(2 * CONV_WIDTH, CONV_WIDTH)
    zv = in_proj(c0 + 2 * RET_WIDTH, RET_WIDTH)
    v_bf = zv.astype(BF16)

    states = [state_ref[hd] for hd in range(RET_HEADS)]
    for n, (c, hd) in enumerate(units):
        rows = slice(c * CHUNK, (c + 1) * CHUNK)
        v = v_bf[rows, hd * HEAD_DIM:(hd + 1) * HEAD_DIM]
        kzt = (kt_rot[hd][:, rows] * zeta_ref[hd]).astype(BF16)
        kv = jnp.dot(kzt, v, preferred_element_type=F32)
        rhs_ref[n, 0:CHUNK, :] = v
        rhs_ref[n, CHUNK:2 * CHUNK, :] = states[hd].astype(BF16)
        states[hd] = cd_ref[hd] * states[hd] + kv
    for hd in range(RET_HEADS):
        state_ref[hd] = states[hd]

    zg = in_proj(c0 + 3 * RET_WIDTH, RET_WIDTH)

    p = cc * ch
    pd1_ref[1:1 + tile, :] = p
    pd2_ref[2:2 + tile, :] = p
    p1 = pd1_ref[0:tile, :]
    p2 = pd2_ref[0:tile, :]
    y = cb * (p2 * convw_ref[0:1, :] + p1 * convw_ref[1:2, :] + p * convw_ref[2:3, :])
    pd1_ref[0:SUBLANES, :] = pd1_ref[tile:tile + SUBLANES, :]
    pd2_ref[0:SUBLANES, :] = pd2_ref[tile:tile + SUBLANES, :]

    lane = lax.broadcasted_iota(jnp.int32, (tile, LANES), 1)
    low = lane < CONV_GROUP_DIM
    for blk in range(CONV_WIDTH // LANES):
        sl = slice(blk * LANES, (blk + 1) * LANES)
        yb = y[:, sl]
        y2 = yb * yb
        ss_lo = jnp.sum(jnp.where(low, y2, 0.0), axis=-1, keepdims=True)
        ss_hi = jnp.sum(jnp.where(low, 0.0, y2), axis=-1, keepdims=True)
        inv = lax.rsqrt(jnp.where(low, ss_lo, ss_hi) * (1.0 / CONV_GROUP_DIM) + NORM_EPS)
        mix_ref[:, sl] = (yb * inv * cng_ref[:, sl]).astype(BF16)

    outs = [jnp.dot(lhs_ref[n], rhs_ref[n], preferred_element_type=F32) for n in range(len(units))]
    for hd in range(RET_HEADS):
        hs = slice(hd * HEAD_DIM, (hd + 1) * HEAD_DIM)
        o = jnp.concatenate([outs[c * RET_HEADS + hd] for c in range(n_chunks)], axis=0)
        gate = zg[:, hs]
        gate = gate * (1.0 / (1.0 + jnp.exp(-gate)))
        yr = o * _rms_scale(o) * rng_ref[:, hs] * gate
        mix_ref[:, CONV_WIDTH + hd * HEAD_DIM:CONV_WIDTH + (hd + 1) * HEAD_DIM] = yr.astype(BF16)

    state_ref[0, 0:SUBLANES, :] = state_ref[0, 0:SUBLANES, :] + warm[0:SUBLANES, 0:LANES]


def _mlp_kernel(x_ref, mix_ref, g2_ref, wout_ref, wup_ref, wdn_ref, gf_ref, o_ref, hid_ref):
    n_blocks = x_ref.shape[0] // MLP_DOWN_ROWS
    us = []
    for r in range(n_blocks):
        rows = slice(r * MLP_DOWN_ROWS, (r + 1) * MLP_DOWN_ROWS)
        h = x_ref[rows, :] + _dot(mix_ref[rows, :], wout_ref[...])
        o_ref[rows, :] = h
        us.append((h * _rms_scale(h) * g2_ref[...]).astype(BF16))
    u = jnp.concatenate(us, axis=0)
    for c in range(D_FF // FF_CHUNK):
        cols = slice(c * FF_CHUNK, (c + 1) * FF_CHUNK)
        a = jnp.maximum(_dot(u, wup_ref[:, cols]), 0.0)
        hid_ref[:, cols] = (a * a).astype(BF16)
    for r in range(n_blocks):
        rows = slice(r * MLP_DOWN_ROWS, (r + 1) * MLP_DOWN_ROWS)
        y = o_ref[rows, :] + _dot(hid_ref[rows, :], wdn_ref[...])
        o_ref[rows, :] = y * _rms_scale(y) * gf_ref[...]


def _resident(shape):
    nd = len(shape)
    return pl.BlockSpec(shape, lambda *_: (0,) * nd, pipeline_mode=pl.Buffered(1))


@functools.lru_cache(maxsize=None)
def _retention_tables(seq):
    half = HEAD_DIM // 2
    inv_freq = 1.0 / (ROPE_BASE ** (np.arange(half, dtype=np.float64) / half))
    ang = np.arange(seq, dtype=np.float64)[:, None] * inv_freq[None, :]
    cos = np.cos(ang)
    sin = np.sin(ang)
    cos_t = np.concatenate([cos, cos], axis=-1)
    sin_t = np.concatenate([-sin, sin], axis=-1)

    log_gamma = np.log(1.0 - 2.0 ** (-5.0 - np.arange(RET_HEADS, dtype=np.float64)))
    idx = np.arange(CHUNK, dtype=np.float64)
    diff = idx[:, None] - idx[None, :]
    intra = np.where(diff[None] >= 0, np.exp(log_gamma[:, None, None] * np.maximum(diff, 0.0)[None]), 0.0)
    zeta = np.exp(log_gamma[:, None] * (CHUNK - 1 - idx)[None])
    xi = np.exp(log_gamma[:, None] * (idx + 1.0)[None])
    chunk_decay = np.exp(log_gamma * CHUNK)
    k_scale = HEAD_DIM ** -0.5
    decay_t = intra * k_scale
    zeta_t = np.broadcast_to((zeta * k_scale)[:, None, :], (RET_HEADS, HEAD_DIM, CHUNK))
    xi_t = np.broadcast_to(xi[:, :, None], (RET_HEADS, CHUNK, HEAD_DIM))
    return tuple(np.ascontiguousarray(t, dtype=np.float32)
                 for t in (cos_t, sin_t, decay_t, zeta_t, xi_t, chunk_decay))


def kernel(x, norm1_g, w_in, conv_w, conv_norm_g, ret_norm_g, w_out, norm2_g, w_up, w_down, final_norm_g):
    batch, seq, d_model = x.shape
    assert d_model == D_MODEL and w_in.shape == (D_MODEL, IN_COLS)
    assert seq % MIXER_TILE == 0 and MIXER_TILE % CHUNK == 0 and (batch * seq) % MLP_TILE == 0

    cos_t, sin_t, decay_t, zeta_t, xi_t, chunk_decay = _retention_tables(seq)
    row = lambda g: g.reshape(1, -1).astype(F32)
    n_units = (MIXER_TILE // CHUNK) * RET_HEADS

    tile_spec = pl.BlockSpec((None, MIXER_TILE, D_MODEL), lambda b, j: (b, j, 0))
    rope_spec = pl.BlockSpec((MIXER_TILE, HEAD_DIM), lambda b, j: (j, 0))
    seq_tiles = seq // MIXER_TILE
    n_steps = batch * seq_tiles
    assert D_MODEL % n_steps == 0 and D_FF % n_steps == 0
    wup_slab = pl.BlockSpec((D_MODEL // n_steps, D_FF), lambda b, j: (b * seq_tiles + j, 0))
    wdn_slab = pl.BlockSpec((D_FF // n_steps, D_MODEL), lambda b, j: (b * seq_tiles + j, 0))
    wout_slab = pl.BlockSpec((D_MODEL // n_steps, D_MODEL), lambda b, j: (b * seq_tiles + j, 0))
    mix, w_up_bf, w_down_bf, w_out_bf = pl.pallas_call(
        _mixer_kernel,
        grid=(batch, seq_tiles),
        in_specs=[
            pl.BlockSpec(memory_space=pltpu.SMEM),
            tile_spec,
            _resident((1, D_MODEL)),
            pl.BlockSpec(memory_space=pl.ANY),
            _resident((CONV_K, CONV_WIDTH)),
            _resident((1, CONV_WIDTH)),
            _resident((1, RET_WIDTH)),
            rope_spec,
            rope_spec,
            _resident((RET_HEADS, CHUNK, CHUNK)),
            _resident((RET_HEADS, CHUNK, HEAD_DIM)),
            _resident((RET_HEADS, CHUNK, HEAD_DIM)),
            wup_slab,
            wdn_slab,
            wout_slab,
        ],
        out_specs=[tile_spec, wup_slab, wdn_slab, wout_slab],
        out_shape=[jax.ShapeDtypeStruct(x.shape, BF16),
                   jax.ShapeDtypeStruct(w_up.shape, BF16),
                   jax.ShapeDtypeStruct(w_down.shape, BF16),
                   jax.ShapeDtypeStruct(w_out.shape, BF16)],
        scratch_shapes=[
            pltpu.VMEM((D_MODEL, IN_COLS), BF16),
            pltpu.VMEM((CAST_SLOTS, CAST_ROWS, IN_COLS), F32),
            pltpu.SemaphoreType.DMA((CAST_SLOTS,)),
            pltpu.VMEM((RET_HEADS, HEAD_DIM, HEAD_DIM), F32),
            pltpu.VMEM((MIXER_TILE + SUBLANES, CONV_WIDTH), F32),
            pltpu.VMEM((MIXER_TILE + SUBLANES, CONV_WIDTH), F32),
            pltpu.VMEM((n_units, CHUNK, 2 * CHUNK), BF16),
            pltpu.VMEM((n_units, 2 * CHUNK, HEAD_DIM), BF16),
        ],
        compiler_params=pltpu.CompilerParams(
            dimension_semantics=("arbitrary", "arbitrary"),
            vmem_limit_bytes=VMEM_LIMIT_BYTES),
        name="mixer",
    )(chunk_decay, x, row(norm1_g), w_in, conv_w, row(conv_norm_g), row(ret_norm_g),
      cos_t, sin_t, decay_t, zeta_t, xi_t, w_up, w_down, w_out)

    tokens = batch * seq
    tok_spec = pl.BlockSpec((MLP_TILE, D_MODEL), lambda i: (i, 0))
    out = pl.pallas_call(
        _mlp_kernel,
        grid=(tokens // MLP_TILE,),
        in_specs=[
            tok_spec,
            tok_spec,
            _resident((1, D_MODEL)),
            _resident((D_MODEL, D_MODEL)),
            _resident((D_MODEL, D_FF)),
            _resident((D_FF, D_MODEL)),
            _resident((1, D_MODEL)),
        ],
        out_specs=tok_spec,
        out_shape=jax.ShapeDtypeStruct((tokens, D_MODEL), F32),
        scratch_shapes=[pltpu.VMEM((MLP_TILE, D_FF), BF16)],
        compiler_params=pltpu.CompilerParams(
            dimension_semantics=("arbitrary",),
            vmem_limit_bytes=VMEM_LIMIT_BYTES),
        name="mlp",
    )(x.reshape(tokens, D_MODEL), mix.reshape(tokens, D_MODEL), row(norm2_g), w_out_bf, w_up_bf, w_down_bf,
      row(final_norm_g))
    return out.reshape(batch, seq, D_MODEL)
```

```python
import functools

import jax
import jax.numpy as jnp
import numpy as np
from jax import lax
from jax.experimental import pallas as pl
from jax.experimental.pallas import tpu as pltpu

D_MODEL = 1024
CONV_WIDTH = 512
CONV_GROUPS = 8
CONV_GROUP_DIM = CONV_WIDTH // CONV_GROUPS
CONV_K = 3
RET_WIDTH = 512
RET_HEADS = 4
HEAD_DIM = RET_WIDTH // RET_HEADS
CHUNK = 128
ROPE_BASE = 10000.0
D_FF = 4 * D_MODEL
NORM_EPS = 1e-6
IN_COLS = 3 * CONV_WIDTH + 4 * RET_WIDTH

LANES = 128
SUBLANES = 8
VMEM_LIMIT_BYTES = 56 * 1024 * 1024

MIXER_TILE = 1024
MLP_TILE = 1024
MLP_OUT_ROWS = 512
MLP_DOWN_ROWS = 256
FF_CHUNK = 1024
WARM_K = 256
CAST_ROWS = 32
CAST_SLOTS = 8

BF16 = jnp.bfloat16
F32 = jnp.float32


def _dot(a, b):
    return jnp.dot(a, b, preferred_element_type=F32)


def _rms_scale(x):
    return lax.rsqrt(jnp.mean(x * x, axis=-1, keepdims=True) + NORM_EPS)


def _bf16_cast_ring(src_hbm, dst_ref, stage_ref, sem_ref):
    slots, rows = stage_ref.shape[0], stage_ref.shape[1]
    n = src_hbm.shape[0] // rows

    def copy(k):
        slot = k % slots
        return pltpu.make_async_copy(src_hbm.at[pl.ds(k * rows, rows), :], stage_ref.at[slot], sem_ref.at[slot])

    def prime():
        for k in range(min(slots, n)):
            copy(k).start()

    def drain():
        for k in range(n):
            copy(k).wait()
            dst_ref[k * rows:(k + 1) * rows, :] = stage_ref[k % slots].astype(BF16)
            if k + slots < n:
                copy(k + slots).start()

    return prime, drain


def _mixer_kernel(cd_ref, x_ref, g1_ref, win_hbm, convw_ref, cng_ref, rng_ref,
                  cos_ref, sin_ref, decay_ref, zeta_ref, xi_ref, wup_ref, wdn_ref, wout_ref,
                  mix_ref, wup_bf_ref, wdn_bf_ref, wout_bf_ref,
                  win_ref, win_stage_ref, win_sem_ref,
                  state_ref, pd1_ref, pd2_ref, lhs_ref, rhs_ref):
    tile = x_ref.shape[0]
    n_chunks = tile // CHUNK
    units = [(c, hd) for c in range(n_chunks) for hd in range(RET_HEADS)]

    @pl.when((pl.program_id(0) == 0) & (pl.program_id(1) == 0))
    def _():
        prime_in, drain_in = _bf16_cast_ring(win_hbm, win_ref, win_stage_ref, win_sem_ref)
        prime_in()
        drain_in()

    wup_bf_ref[...] = wup_ref[...].astype(BF16)
    wdn_bf_ref[...] = wdn_ref[...].astype(BF16)
    wout_bf_ref[...] = wout_ref[...].astype(BF16)

    @pl.when(pl.program_id(1) == 0)
    def _():
        state_ref[...] = jnp.zeros_like(state_ref)
        zeros = jnp.zeros((SUBLANES, CONV_WIDTH), F32)
        pd1_ref[0:SUBLANES, :] = zeros
        pd2_ref[0:SUBLANES, :] = zeros
        pd1_ref[tile:tile + SUBLANES, :] = zeros
        pd2_ref[tile:tile + SUBLANES, :] = zeros

    warm = jnp.dot(jnp.zeros((tile // 2, WARM_K), BF16), win_ref[0:WARM_K, 0:2 * LANES],
                   preferred_element_type=F32)

    x = x_ref[...]
    u = (x * _rms_scale(x) * g1_ref[...]).astype(BF16)

    def in_proj(col0, width):
        return jnp.dot(u, win_ref[:, col0:col0 + width], preferred_element_type=F32)

    c0 = 3 * CONV_WIDTH

    zq = in_proj(c0, RET_WIDTH)
    zk = in_proj(c0 + RET_WIDTH, RET_WIDTH)
    cb = in_proj(0, CONV_WIDTH)
    cc = in_proj(CONV_WIDTH, CONV_WIDTH)

    cos = cos_ref[...]
    sin = sin_ref[...]
    q_rot, kt_rot = [], []
    for hd in range(RET_HEADS):
        hs = slice(hd * HEAD_DIM, (hd + 1) * HEAD_DIM)
        qh = zq[:, hs]
        kh = zk[:, hs]
        q_rot.append(qh * cos + pltpu.roll(qh, HEAD_DIM // 2, axis=1) * sin)
        kt_rot.append((kh * cos + pltpu.roll(kh, HEAD_DIM // 2, axis=1) * sin).T)

    for n, (c, hd) in enumerate(units):
        rows = slice(c * CHUNK, (c + 1) * CHUNK)
        q = q_rot[hd][rows]
        kt = kt_rot[hd][:, rows]
        s = jnp.dot(q.astype(BF16), kt.astype(BF16), preferred_element_type=F32) * decay_ref[hd]
        lhs_ref[n, :, 0:CHUNK] = s.astype(BF16)
        lhs_ref[n, :, CHUNK:2 * CHUNK] = (q * xi_ref[hd]).astype(BF16)

    ch = in_proj(2 * CONV_WIDTH, CONV_WIDTH)
    zv = in_proj(c0 + 2 * RET_WIDTH, RET_WIDTH)
    v_bf = zv.astype(BF16)

    states = [state_ref[hd] for hd in range(RET_HEADS)]
    for n, (c, hd) in enumerate(units):
        rows = slice(c * CHUNK, (c + 1) * CHUNK)
        v = v_bf[rows, hd * HEAD_DIM:(hd + 1) * HEAD_DIM]
        kzt = (kt_rot[hd][:, rows] * zeta_ref[hd]).astype(BF16)
        kv = jnp.dot(kzt, v, preferred_element_type=F32)
        rhs_ref[n, 0:CHUNK, :] = v
        rhs_ref[n, CHUNK:2 * CHUNK, :] = states[hd].astype(BF16)
        states[hd] = cd_ref[hd] * states[hd] + kv
    for hd in range(RET_HEADS):
        state_ref[hd] = states[hd]

    zg = in_proj(c0 + 3 * RET_WIDTH, RET_WIDTH)

    p = cc * ch
    pd1_ref[1:1 + tile, :] = p
    pd2_ref[2:2 + tile, :] = p
    p1 = pd1_ref[0:tile, :]
    p2 = pd2_ref[0:tile, :]
    y = cb * (p2 * convw_ref[0:1, :] + p1 * convw_ref[1:2, :] + p * convw_ref[2:3, :])
    pd1_ref[0:SUBLANES, :] = pd1_ref[tile:tile + SUBLANES, :]
    pd2_ref[0:SUBLANES, :] = pd2_ref[tile:tile + SUBLANES, :]

    lane = lax.broadcasted_iota(jnp.int32, (tile, LANES), 1)
    low = lane < CONV_GROUP_DIM
    for blk in range(CONV_WIDTH // LANES):
        sl = slice(blk * LANES, (blk + 1) * LANES)
        yb = y[:, sl]
        y2 = yb * yb
        ss_lo = jnp.sum(jnp.where(low, y2, 0.0), axis=-1, keepdims=True)
        ss_hi = jnp.sum(jnp.where(low, 0.0, y2), axis=-1, keepdims=True)
        inv = lax.rsqrt(jnp.where(low, ss_lo, ss_hi) * (1.0 / CONV_GROUP_DIM) + NORM_EPS)
        mix_ref[:, sl] = (yb * inv * cng_ref[:, sl]).astype(BF16)

    outs = [jnp.dot(lhs_ref[n], rhs_ref[n], preferred_element_type=F32) for n in range(len(units))]
    for hd in range(RET_HEADS):
        hs = slice(hd * HEAD_DIM, (hd + 1) * HEAD_DIM)
        o = jnp.concatenate([outs[c * RET_HEADS + hd] for c in range(n_chunks)], axis=0)
        gate = zg[:, hs]
        gate = gate * (1.0 / (1.0 + jnp.exp(-gate)))
        yr = o * _rms_scale(o) * rng_ref[:, hs] * gate
        mix_ref[:, CONV_WIDTH + hd * HEAD_DIM:CONV_WIDTH + (hd + 1) * HEAD_DIM] = yr.astype(BF16)

    state_ref[0, 0:SUBLANES, :] = state_ref[0, 0:SUBLANES, :] + warm[0:SUBLANES, 0:LANES]


def _mlp_kernel(x_ref, mix_ref, g2_ref, wout_ref, wup_ref, wdn_ref, gf_ref, o_ref, hid_ref):
    n_blocks = x_ref.shape[0] // MLP_DOWN_ROWS
    us = []
    for r in range(x_ref.shape[0] // MLP_OUT_ROWS):
        rows = slice(r * MLP_OUT_ROWS, (r + 1) * MLP_OUT_ROWS)
        h = x_ref[rows, :] + _dot(mix_ref[rows, :], wout_ref[...])
        o_ref[rows, :] = h
        us.append((h * _rms_scale(h) * g2_ref[...]).astype(BF16))
    u = jnp.concatenate(us, axis=0)
    for c in range(D_FF // FF_CHUNK):
        cols = slice(c * FF_CHUNK, (c + 1) * FF_CHUNK)
        a = jnp.maximum(_dot(u, wup_ref[:, cols]), 0.0)
        hid_ref[:, cols] = (a * a).astype(BF16)
    for r in range(n_blocks):
        rows = slice(r * MLP_DOWN_ROWS, (r + 1) * MLP_DOWN_ROWS)
        y = o_ref[rows, :] + _dot(hid_ref[rows, :], wdn_ref[...])
        o_ref[rows, :] = y * _rms_scale(y) * gf_ref[...]


def _resident(shape):
    nd = len(shape)
    return pl.BlockSpec(shape, lambda *_: (0,) * nd, pipeline_mode=pl.Buffered(1))


@functools.lru_cache(maxsize=None)
def _retention_tables(seq):
    half = HEAD_DIM // 2
    inv_freq = 1.0 / (ROPE_BASE ** (np.arange(half, dtype=np.float64) / half))
    ang = np.arange(seq, dtype=np.float64)[:, None] * inv_freq[None, :]
    cos = np.cos(ang)
    sin = np.sin(ang)
    cos_t = np.concatenate([cos, cos], axis=-1)
    sin_t = np.concatenate([-sin, sin], axis=-1)

    log_gamma = np.log(1.0 - 2.0 ** (-5.0 - np.arange(RET_HEADS, dtype=np.float64)))
    idx = np.arange(CHUNK, dtype=np.float64)
    diff = idx[:, None] - idx[None, :]
    intra = np.where(diff[None] >= 0, np.exp(log_gamma[:, None, None] * np.maximum(diff, 0.0)[None]), 0.0)
    zeta = np.exp(log_gamma[:, None] * (CHUNK - 1 - idx)[None])
    xi = np.exp(log_gamma[:, None] * (idx + 1.0)[None])
    chunk_decay = np.exp(log_gamma * CHUNK)
    k_scale = HEAD_DIM ** -0.5
    decay_t = intra * k_scale
    zeta_t = np.broadcast_to((zeta * k_scale)[:, None, :], (RET_HEADS, HEAD_DIM, CHUNK))
    xi_t = np.broadcast_to(xi[:, :, None], (RET_HEADS, CHUNK, HEAD_DIM))
    return tuple(np.ascontiguousarray(t, dtype=np.float32)
                 for t in (cos_t, sin_t, decay_t, zeta_t, xi_t, chunk_decay))


def kernel(x, norm1_g, w_in, conv_w, conv_norm_g, ret_norm_g, w_out, norm2_g, w_up, w_down, final_norm_g):
    batch, seq, d_model = x.shape
    assert d_model == D_MODEL and w_in.shape == (D_MODEL, IN_COLS)
    assert seq % MIXER_TILE == 0 and MIXER_TILE % CHUNK == 0 and (batch * seq) % MLP_TILE == 0

    cos_t, sin_t, decay_t, zeta_t, xi_t, chunk_decay = _retention_tables(seq)
    row = lambda g: g.reshape(1, -1).astype(F32)
    n_units = (MIXER_TILE // CHUNK) * RET_HEADS

    tile_spec = pl.BlockSpec((None, MIXER_TILE, D_MODEL), lambda b, j: (b, j, 0))
    rope_spec = pl.BlockSpec((MIXER_TILE, HEAD_DIM), lambda b, j: (j, 0))
    seq_tiles = seq // MIXER_TILE
    n_steps = batch * seq_tiles
    assert D_MODEL % n_steps == 0 and D_FF % n_steps == 0
    wup_slab = pl.BlockSpec((D_MODEL // n_steps, D_FF), lambda b, j: (b * seq_tiles + j, 0))
    wdn_slab = pl.BlockSpec((D_FF // n_steps, D_MODEL), lambda b, j: (b * seq_tiles + j, 0))
    wout_slab = pl.BlockSpec((D_MODEL // n_steps, D_MODEL), lambda b, j: (b * seq_tiles + j, 0))
    mix, w_up_bf, w_down_bf, w_out_bf = pl.pallas_call(
        _mixer_kernel,
        grid=(batch, seq_tiles),
        in_specs=[
            pl.BlockSpec(memory_space=pltpu.SMEM),
            tile_spec,
            _resident((1, D_MODEL)),
            pl.BlockSpec(memory_space=pl.ANY),
            _resident((CONV_K, CONV_WIDTH)),
            _resident((1, CONV_WIDTH)),
            _resident((1, RET_WIDTH)),
            rope_spec,
            rope_spec,
            _resident((RET_HEADS, CHUNK, CHUNK)),
            _resident((RET_HEADS, CHUNK, HEAD_DIM)),
            _resident((RET_HEADS, CHUNK, HEAD_DIM)),
            wup_slab,
            wdn_slab,
            wout_slab,
        ],
        out_specs=[tile_spec, wup_slab, wdn_slab, wout_slab],
        out_shape=[jax.ShapeDtypeStruct(x.shape, BF16),
                   jax.ShapeDtypeStruct(w_up.shape, BF16),
                   jax.ShapeDtypeStruct(w_down.shape, BF16),
                   jax.ShapeDtypeStruct(w_out.shape, BF16)],
        scratch_shapes=[
            pltpu.VMEM((D_MODEL, IN_COLS), BF16),
            pltpu.VMEM((CAST_SLOTS, CAST_ROWS, IN_COLS), F32),
            pltpu.SemaphoreType.DMA((CAST_SLOTS,)),
            pltpu.VMEM((RET_HEADS, HEAD_DIM, HEAD_DIM), F32),
            pltpu.VMEM((MIXER_TILE + SUBLANES, CONV_WIDTH), F32),
            pltpu.VMEM((MIXER_TILE + SUBLANES, CONV_WIDTH), F32),
            pltpu.VMEM((n_units, CHUNK, 2 * CHUNK), BF16),
            pltpu.VMEM((n_units, 2 * CHUNK, HEAD_DIM), BF16),
        ],
        compiler_params=pltpu.CompilerParams(
            dimension_semantics=("arbitrary", "arbitrary"),
            vmem_limit_bytes=VMEM_LIMIT_BYTES),
        name="mixer",
    )(chunk_decay, x, row(norm1_g), w_in, conv_w, row(conv_norm_g), row(ret_norm_g),
      cos_t, sin_t, decay_t, zeta_t, xi_t, w_up, w_down, w_out)

    tokens = batch * seq
    tok_spec = pl.BlockSpec((MLP_TILE, D_MODEL), lambda i: (i, 0))
    out = pl.pallas_call(
        _mlp_kernel,
        grid=(tokens // MLP_TILE,),
        in_specs=[
            tok_spec,
            tok_spec,
            _resident((1, D_MODEL)),
            _resident((D_MODEL, D_MODEL)),
            _resident((D_MODEL, D_FF)),
            _resident((D_FF, D_MODEL)),
            _resident((1, D_MODEL)),
        ],
        out_specs=tok_spec,
        out_shape=jax.ShapeDtypeStruct((tokens, D_MODEL), F32),
        scratch_shapes=[pltpu.VMEM((MLP_TILE, D_FF), BF16)],
        compiler_params=pltpu.CompilerParams(
            dimension_semantics=("arbitrary",),
            vmem_limit_bytes=VMEM_LIMIT_BYTES),
        name="mlp",
    )(x.reshape(tokens, D_MODEL), mix.reshape(tokens, D_MODEL), row(norm2_g), w_out_bf, w_up_bf, w_down_bf,
      row(final_norm_g))
    return out.reshape(batch, seq, D_MODEL)
```

```python
import functools

import jax
import jax.numpy as jnp
import numpy as np
from jax import lax
from jax.experimental import pallas as pl
from jax.experimental.pallas import tpu as pltpu

D_MODEL = 1024
CONV_WIDTH = 512
CONV_GROUPS = 8
CONV_GROUP_DIM = CONV_WIDTH // CONV_GROUPS
CONV_K = 3
RET_WIDTH = 512
RET_HEADS = 4
HEAD_DIM = RET_WIDTH // RET_HEADS
CHUNK = 128
ROPE_BASE = 10000.0
D_FF = 4 * D_MODEL
NORM_EPS = 1e-6
IN_COLS = 3 * CONV_WIDTH + 4 * RET_WIDTH

LANES = 128
SUBLANES = 8
VMEM_LIMIT_BYTES = 56 * 1024 * 1024

MIXER_TILE = 1024
MLP_TILE = 1024
MLP_OUT_ROWS = 1024
MLP_DOWN_ROWS = 256
FF_CHUNK = 1024
WARM_K = 256
CAST_ROWS = 32
CAST_SLOTS = 8

BF16 = jnp.bfloat16
F32 = jnp.float32


def _dot(a, b):
    return jnp.dot(a, b, preferred_element_type=F32)


def _rms_scale(x):
    return lax.rsqrt(jnp.mean(x * x, axis=-1, keepdims=True) + NORM_EPS)


def _bf16_cast_ring(src_hbm, dst_ref, stage_ref, sem_ref):
    slots, rows = stage_ref.shape[0], stage_ref.shape[1]
    n = src_hbm.shape[0] // rows

    def copy(k):
        slot = k % slots
        return pltpu.make_async_copy(src_hbm.at[pl.ds(k * rows, rows), :], stage_ref.at[slot], sem_ref.at[slot])

    def prime():
        for k in range(min(slots, n)):
            copy(k).start()

    def drain():
        for k in range(n):
            copy(k).wait()
            dst_ref[k * rows:(k + 1) * rows, :] = stage_ref[k % slots].astype(BF16)
            if k + slots < n:
                copy(k + slots).start()

    return prime, drain


def _mixer_kernel(cd_ref, x_ref, g1_ref, win_hbm, convw_ref, cng_ref, rng_ref,
                  cos_ref, sin_ref, decay_ref, zeta_ref, xi_ref, wup_ref, wdn_ref, wout_ref,
                  mix_ref, wup_bf_ref, wdn_bf_ref, wout_bf_ref,
                  win_ref, win_stage_ref, win_sem_ref,
                  state_ref, pd1_ref, pd2_ref, lhs_ref, rhs_ref):
    tile = x_ref.shape[0]
    n_chunks = tile // CHUNK
    units = [(c, hd) for c in range(n_chunks) for hd in range(RET_HEADS)]

    @pl.when((pl.program_id(0) == 0) & (pl.program_id(1) == 0))
    def _():
        prime_in, drain_in = _bf16_cast_ring(win_hbm, win_ref, win_stage_ref, win_sem_ref)
        prime_in()
        drain_in()

    wup_bf_ref[...] = wup_ref[...].astype(BF16)
    wdn_bf_ref[...] = wdn_ref[...].astype(BF16)
    wout_bf_ref[...] = wout_ref[...].astype(BF16)

    @pl.when(pl.program_id(1) == 0)
    def _():
        state_ref[...] = jnp.zeros_like(state_ref)
        zeros = jnp.zeros((SUBLANES, CONV_WIDTH), F32)
        pd1_ref[0:SUBLANES, :] = zeros
        pd2_ref[0:SUBLANES, :] = zeros
        pd1_ref[tile:tile + SUBLANES, :] = zeros
        pd2_ref[tile:tile + SUBLANES, :] = zeros

    warm = jnp.dot(jnp.zeros((tile // 2, WARM_K), BF16), win_ref[0:WARM_K, 0:2 * LANES],
                   preferred_element_type=F32)

    x = x_ref[...]
    u = (x * _rms_scale(x) * g1_ref[...]).astype(BF16)

    def in_proj(col0, width):
        return jnp.dot(u, win_ref[:, col0:col0 + width], preferred_element_type=F32)

    c0 = 3 * CONV_WIDTH

    zq = in_proj(c0, RET_WIDTH)
    zk = in_proj(c0 + RET_WIDTH, RET_WIDTH)
    cb = in_proj(0, CONV_WIDTH)
    cc = in_proj(CONV_WIDTH, CONV_WIDTH)

    cos = cos_ref[...]
    sin = sin_ref[...]
    q_rot, kt_rot = [], []
    for hd in range(RET_HEADS):
        hs = slice(hd * HEAD_DIM, (hd + 1) * HEAD_DIM)
        qh = zq[:, hs]
        kh = zk[:, hs]
        q_rot.append(qh * cos + pltpu.roll(qh, HEAD_DIM // 2, axis=1) * sin)
        kt_rot.append((kh * cos + pltpu.roll(kh, HEAD_DIM // 2, axis=1) * sin).T)

    for n, (c, hd) in enumerate(units):
        rows = slice(c * CHUNK, (c + 1) * CHUNK)
        q = q_rot[hd][rows]
        kt = kt_rot[hd][:, rows]
        s = jnp.dot(q.astype(BF16), kt.astype(BF16), preferred_element_type=F32) * decay_ref[hd]
        lhs_ref[n, :, 0:CHUNK] = s.astype(BF16)
        lhs_ref[n, :, CHUNK:2 * CHUNK] = (q * xi_ref[hd]).astype(BF16)

    ch = in_proj(2 * CONV_WIDTH, CONV_WIDTH)
    zv = in_proj(c0 + 2 * RET_WIDTH, RET_WIDTH)
    v_bf = zv.astype(BF16)

    states = [state_ref[hd] for hd in range(RET_HEADS)]
    for n, (c, hd) in enumerate(units):
        rows = slice(c * CHUNK, (c + 1) * CHUNK)
        v = v_bf[rows, hd * HEAD_DIM:(hd + 1) * HEAD_DIM]
        kzt = (kt_rot[hd][:, rows] * zeta_ref[hd]).astype(BF16)
        kv = jnp.dot(kzt, v, preferred_element_type=F32)
        rhs_ref[n, 0:CHUNK, :] = v
        rhs_ref[n, CHUNK:2 * CHUNK, :] = states[hd].astype(BF16)
        states[hd] = cd_ref[hd] * states[hd] + kv
    for hd in range(RET_HEADS):
        state_ref[hd] = states[hd]

    zg = in_proj(c0 + 3 * RET_WIDTH, RET_WIDTH)

    p = cc * ch
    pd1_ref[1:1 + tile, :] = p
    pd2_ref[2:2 + tile, :] = p
    p1 = pd1_ref[0:tile, :]
    p2 = pd2_ref[0:tile, :]
    y = cb * (p2 * convw_ref[0:1, :] + p1 * convw_ref[1:2, :] + p * convw_ref[2:3, :])
    pd1_ref[0:SUBLANES, :] = pd1_ref[tile:tile + SUBLANES, :]
    pd2_ref[0:SUBLANES, :] = pd2_ref[tile:tile + SUBLANES, :]

    lane = lax.broadcasted_iota(jnp.int32, (tile, LANES), 1)
    low = lane < CONV_GROUP_DIM
    for blk in range(CONV_WIDTH // LANES):
        sl = slice(blk * LANES, (blk + 1) * LANES)
        yb = y[:, sl]
        y2 = yb * yb
        ss_lo = jnp.sum(jnp.where(low, y2, 0.0), axis=-1, keepdims=True)
        ss_hi = jnp.sum(jnp.where(low, 0.0, y2), axis=-1, keepdims=True)
        inv = lax.rsqrt(jnp.where(low, ss_lo, ss_hi) * (1.0 / CONV_GROUP_DIM) + NORM_EPS)
        mix_ref[:, sl] = (yb * inv * cng_ref[:, sl]).astype(BF16)

    outs = [jnp.dot(lhs_ref[n], rhs_ref[n], preferred_element_type=F32) for n in range(len(units))]
    for hd in range(RET_HEADS):
        hs = slice(hd * HEAD_DIM, (hd + 1) * HEAD_DIM)
        o = jnp.concatenate([outs[c * RET_HEADS + hd] for c in range(n_chunks)], axis=0)
        gate = zg[:, hs]
        gate = gate * (1.0 / (1.0 + jnp.exp(-gate)))
        yr = o * _rms_scale(o) * rng_ref[:, hs] * gate
        mix_ref[:, CONV_WIDTH + hd * HEAD_DIM:CONV_WIDTH + (hd + 1) * HEAD_DIM] = yr.astype(BF16)

    state_ref[0, 0:SUBLANES, :] = state_ref[0, 0:SUBLANES, :] + warm[0:SUBLANES, 0:LANES]


def _mlp_kernel(x_ref, mix_ref, g2_ref, wout_ref, wup_ref, wdn_ref, gf_ref, o_ref, hid_ref):
    n_blocks = x_ref.shape[0] // MLP_DOWN_ROWS
    us = []
    for r in range(x_ref.shape[0] // MLP_OUT_ROWS):
        rows = slice(r * MLP_OUT_ROWS, (r + 1) * MLP_OUT_ROWS)
        h = x_ref[rows, :] + _dot(mix_ref[rows, :], wout_ref[...])
        o_ref[rows, :] = h
        us.append((h * _rms_scale(h) * g2_ref[...]).astype(BF16))
    u = jnp.concatenate(us, axis=0)
    for c in range(D_FF // FF_CHUNK):
        cols = slice(c * FF_CHUNK, (c + 1) * FF_CHUNK)
        a = jnp.maximum(_dot(u, wup_ref[:, cols]), 0.0)
        hid_ref[:, cols] = (a * a).astype(BF16)
    for r in range(n_blocks):
        rows = slice(r * MLP_DOWN_ROWS, (r + 1) * MLP_DOWN_ROWS)
        y = o_ref[rows, :] + _dot(hid_ref[rows, :], wdn_ref[...])
        o_ref[rows, :] = y * _rms_scale(y) * gf_ref[...]


def _resident(shape):
    nd = len(shape)
    return pl.BlockSpec(shape, lambda *_: (0,) * nd, pipeline_mode=pl.Buffered(1))


@functools.lru_cache(maxsize=None)
def _retention_tables(seq):
    half = HEAD_DIM // 2
    inv_freq = 1.0 / (ROPE_BASE ** (np.arange(half, dtype=np.float64) / half))
    ang = np.arange(seq, dtype=np.float64)[:, None] * inv_freq[None, :]
    cos = np.cos(ang)
    sin = np.sin(ang)
    cos_t = np.concatenate([cos, cos], axis=-1)
    sin_t = np.concatenate([-sin, sin], axis=-1)

    log_gamma = np.log(1.0 - 2.0 ** (-5.0 - np.arange(RET_HEADS, dtype=np.float64)))
    idx = np.arange(CHUNK, dtype=np.float64)
    diff = idx[:, None] - idx[None, :]
    intra = np.where(diff[None] >= 0, np.exp(log_gamma[:, None, None] * np.maximum(diff, 0.0)[None]), 0.0)
    zeta = np.exp(log_gamma[:, None] * (CHUNK - 1 - idx)[None])
    xi = np.exp(log_gamma[:, None] * (idx + 1.0)[None])
    chunk_decay = np.exp(log_gamma * CHUNK)
    k_scale = HEAD_DIM ** -0.5
    decay_t = intra * k_scale
    zeta_t = np.broadcast_to((zeta * k_scale)[:, None, :], (RET_HEADS, HEAD_DIM, CHUNK))
    xi_t = np.broadcast_to(xi[:, :, None], (RET_HEADS, CHUNK, HEAD_DIM))
    return tuple(np.ascontiguousarray(t, dtype=np.float32)
                 for t in (cos_t, sin_t, decay_t, zeta_t, xi_t, chunk_decay))


def kernel(x, norm1_g, w_in, conv_w, conv_norm_g, ret_norm_g, w_out, norm2_g, w_up, w_down, final_norm_g):
    batch, seq, d_model = x.shape
    assert d_model == D_MODEL and w_in.shape == (D_MODEL, IN_COLS)
    assert seq % MIXER_TILE == 0 and MIXER_TILE % CHUNK == 0 and (batch * seq) % MLP_TILE == 0

    cos_t, sin_t, decay_t, zeta_t, xi_t, chunk_decay = _retention_tables(seq)
    row = lambda g: g.reshape(1, -1).astype(F32)
    n_units = (MIXER_TILE // CHUNK) * RET_HEADS

    tile_spec = pl.BlockSpec((None, MIXER_TILE, D_MODEL), lambda b, j: (b, j, 0))
    rope_spec = pl.BlockSpec((MIXER_TILE, HEAD_DIM), lambda b, j: (j, 0))
    seq_tiles = seq // MIXER_TILE
    n_steps = batch * seq_tiles
    assert D_MODEL % n_steps == 0 and D_FF % n_steps == 0
    wup_slab = pl.BlockSpec((D_MODEL // n_steps, D_FF), lambda b, j: (b * seq_tiles + j, 0))
    wdn_slab = pl.BlockSpec((D_FF // n_steps, D_MODEL), lambda b, j: (b * seq_tiles + j, 0))
    wout_slab = pl.BlockSpec((D_MODEL // n_steps, D_MODEL), lambda b, j: (b * seq_tiles + j, 0))
    mix, w_up_bf, w_down_bf, w_out_bf = pl.pallas_call(
        _mixer_kernel,
        grid=(batch, seq_tiles),
        in_specs=[
            pl.BlockSpec(memory_space=pltpu.SMEM),
            tile_spec,
            _resident((1, D_MODEL)),
            pl.BlockSpec(memory_space=pl.ANY),
            _resident((CONV_K, CONV_WIDTH)),
            _resident((1, CONV_WIDTH)),
            _resident((1, RET_WIDTH)),
            rope_spec,
            rope_spec,
            _resident((RET_HEADS, CHUNK, CHUNK)),
            _resident((RET_HEADS, CHUNK, HEAD_DIM)),
            _resident((RET_HEADS, CHUNK, HEAD_DIM)),
            wup_slab,
            wdn_slab,
            wout_slab,
        ],
        out_specs=[tile_spec, wup_slab, wdn_slab, wout_slab],
        out_shape=[jax.ShapeDtypeStruct(x.shape, BF16),
                   jax.ShapeDtypeStruct(w_up.shape, BF16),
                   jax.ShapeDtypeStruct(w_down.shape, BF16),
                   jax.ShapeDtypeStruct(w_out.shape, BF16)],
        scratch_shapes=[
            pltpu.VMEM((D_MODEL, IN_COLS), BF16),
            pltpu.VMEM((CAST_SLOTS, CAST_ROWS, IN_COLS), F32),
            pltpu.SemaphoreType.DMA((CAST_SLOTS,)),
            pltpu.VMEM((RET_HEADS, HEAD_DIM, HEAD_DIM), F32),
            pltpu.VMEM((MIXER_TILE + SUBLANES, CONV_WIDTH), F32),
            pltpu.VMEM((MIXER_TILE + SUBLANES, CONV_WIDTH), F32),
            pltpu.VMEM((n_units, CHUNK, 2 * CHUNK), BF16),
            pltpu.VMEM((n_units, 2 * CHUNK, HEAD_DIM), BF16),
        ],
        compiler_params=pltpu.CompilerParams(
            dimension_semantics=("arbitrary", "arbitrary"),
            vmem_limit_bytes=VMEM_LIMIT_BYTES),
        name="mixer",
    )(chunk_decay, x, row(norm1_g), w_in, conv_w, row(conv_norm_g), row(ret_norm_g),
      cos_t, sin_t, decay_t, zeta_t, xi_t, w_up, w_down, w_out)

    tokens = batch * seq
    tok_spec = pl.BlockSpec((MLP_TILE, D_MODEL), lambda i: (i, 0))
    out = pl.pallas_call(
        _mlp_kernel,
        grid=(tokens // MLP_TILE,),
        in_specs=[
            tok_spec,
            tok_spec,
            _resident((1, D_MODEL)),
            _resident((D_MODEL, D_MODEL)),
            _resident((D_MODEL, D_FF)),
            _resident((D_FF, D_MODEL)),
            _resident((1, D_MODEL)),
        ],
        out_specs=tok_spec,
        out_shape=jax.ShapeDtypeStruct((tokens, D_MODEL), F32),
        scratch_shapes=[pltpu.VMEM((MLP_TILE, D_FF), BF16)],
        compiler_params=pltpu.CompilerParams(
            dimension_semantics=("arbitrary",),
            vmem_limit_bytes=VMEM_LIMIT_BYTES),
        name="mlp",
    )(x.reshape(tokens, D_MODEL), mix.reshape(tokens, D_MODEL), row(norm2_g), w_out_bf, w_up_bf, w_down_bf,
      row(final_norm_g))
    return out.reshape(batch, seq, D_MODEL)
```
